```python
import math
import jax, jax.numpy as jnp
from jax import lax
import numpy as np

D_MODEL = 2048
BATCH = 4
SEQ = 4096
DEPTH = 4

N_META = 16
CHUNK = 128
PAD_FRONT = CHUNK - N_META
A_HEADS = D_MODEL // 256
A_QK_DIM = 64
A_V_DIM = 128
D_A = A_HEADS * A_V_DIM
R_HEADS = D_MODEL // 512
R_QK_DIM = 128
R_V_DIM = 256
D_R = R_HEADS * R_V_DIM
D_MIX = D_A + D_R
COL_SIZES = (A_HEADS * 2 * A_QK_DIM, A_HEADS * 2 * A_QK_DIM, D_A,
             R_HEADS * R_QK_DIM, R_HEADS * R_QK_DIM, D_R, D_R)
SPLITS = tuple(int(s) for s in np.cumsum(COL_SIZES)[:-1])
D_IN = int(sum(COL_SIZES))
D_FF = 256 * math.ceil(8 * D_MODEL / 3 / 256)
CONV_W = 3
EPS = 1e-6
NEG_INF = -1e30

kernel_name = "hymba_diffattn_retnet_convffn"


def rmsnorm(x, g):
    xf = x.astype(jnp.float32)
    y = xf * lax.rsqrt(jnp.mean(xf * xf, axis=-1, keepdims=True) + EPS)
    return (y * g.astype(jnp.float32)).astype(x.dtype)


def diff_attention(q, k, v, lam, slopes, valid):
    B, L, H, _, dqk = q.shape
    dv = v.shape[-1]
    scale = dqk ** -0.5
    kpos = jnp.arange(L)

    def block(start):
        qb = lax.dynamic_slice_in_dim(q, start, CHUNK, axis=1)
        s = jnp.einsum('bqhmd,bkhmd->bhmqk', qb, k).astype(jnp.float32) * scale
        qpos = start + jnp.arange(CHUNK)
        dist = (qpos[:, None] - kpos[None, :])
        bias = -slopes[:, None, None] * dist.astype(jnp.float32)
        mask = (dist >= 0) & valid[None, :]
        s = jnp.where(mask[None, None, None], s + bias[None, :, None], NEG_INF)
        p = jax.nn.softmax(s, axis=-1)
        a = (p[:, :, 0] - lam * p[:, :, 1]).astype(v.dtype)
        return jnp.einsum('bhqk,bkhe->bqhe', a, v)

    starts = jnp.arange(L // CHUNK) * CHUNK
    out = lax.map(block, starts)
    return jnp.transpose(out, (1, 0, 2, 3, 4)).reshape(B, L, H, dv)


def retention(q, k, v, log_g):
    B, L, H, dk = q.shape
    dv = v.shape[-1]
    N = L // CHUNK
    dt = q.dtype
    q = q.reshape(B, N, CHUNK, H, dk)
    k = k.reshape(B, N, CHUNK, H, dk) * (dk ** -0.5)
    v = v.reshape(B, N, CHUNK, H, dv)
    idx = jnp.arange(CHUNK, dtype=jnp.float32)
    diff = idx[:, None] - idx[None, :]
    decay_in = jnp.where(diff >= 0, jnp.exp(log_g[:, None, None] * jnp.maximum(diff, 0.0)), 0.0)
    s = jnp.einsum('bnihd,bnjhd->bnhij', q, k) * decay_in.astype(dt)[None, None]
    intra = jnp.einsum('bnhij,bnjhe->bnihe', s, v)
    k_dec = k * jnp.exp(log_g[None, :] * (CHUNK - 1 - idx)[:, None]).astype(dt)[None, None, :, :, None]
    kv = jnp.einsum('bnjhd,bnjhe->nbhde', k_dec, v)
    g_chunk = jnp.exp(log_g * CHUNK).astype(kv.dtype)[None, :, None, None]

    def step(S, kv_n):
        return g_chunk * S + kv_n, S

    _, S_prev = lax.scan(step, jnp.zeros((B, H, dk, dv), kv.dtype), kv)
    q_dec = q * jnp.exp(log_g[None, :] * (idx + 1.0)[:, None]).astype(dt)[None, None, :, :, None]
    cross = jnp.einsum('bnihd,nbhde->bnihe', q_dec, S_prev)
    return (intra + cross).reshape(B, L, H, dv)


def conv_glu(u, w_gate, w_up, conv_w, w_down):
    L = u.shape[1]
    g = u @ w_gate
    gp = jnp.pad(g, ((0, 0), (CONV_W - 1, 0), (0, 0)))
    gc = gp[:, 0:L] * conv_w[0]
    for i in range(1, CONV_W):
        gc = gc + gp[:, i:i + L] * conv_w[i]
    return (jax.nn.silu(gc) * (u @ w_up)) @ w_down


def setup_inputs(seed: int = 0) -> dict:
    key = jax.random.key(seed)
    ks = jax.random.split(key, 14)
    f32 = jnp.float32
    nrm = lambda k, shape, s: jax.random.normal(k, shape, f32) * s
    return {
        "x": nrm(ks[0], (BATCH, SEQ, D_MODEL), 1.0),
        "meta_tokens": nrm(ks[1], (N_META, D_MODEL), 1.0),
        "attn_norm": 1.0 + nrm(ks[2], (DEPTH, D_MODEL), 0.02),
        "w_in": nrm(ks[3], (DEPTH, D_MODEL, D_IN), D_MODEL ** -0.5),
        "lambda_qk": nrm(ks[4], (DEPTH, 4, A_QK_DIM), 0.1),
        "attn_subln": 1.0 + nrm(ks[5], (DEPTH, A_V_DIM), 0.02),
        "ret_norm": 1.0 + nrm(ks[6], (DEPTH, R_V_DIM), 0.02),
        "w_out": nrm(ks[7], (DEPTH, D_MIX, D_MODEL), D_MIX ** -0.5),
        "ffn_norm": 1.0 + nrm(ks[8], (DEPTH, D_MODEL), 0.02),
        "w_gate": nrm(ks[9], (DEPTH, D_MODEL, D_FF), D_MODEL ** -0.5),
        "w_up": nrm(ks[10], (DEPTH, D_MODEL, D_FF), D_MODEL ** -0.5),
        "conv_w": nrm(ks[11], (DEPTH, CONV_W, D_FF), CONV_W ** -0.5),
        "w_down": nrm(ks[12], (DEPTH, D_FF, D_MODEL), D_FF ** -0.5),
        "final_norm": 1.0 + nrm(ks[13], (D_MODEL,), 0.02),
    }


def reference(x, meta_tokens, attn_norm, w_in, lambda_qk, attn_subln, ret_norm, w_out,
              ffn_norm, w_gate, w_up, conv_w, w_down, final_norm):
    B, S, D = x.shape
    L = CHUNK + S
    pad = jnp.zeros((B, PAD_FRONT, D), x.dtype)
    meta = jnp.broadcast_to(meta_tokens.astype(x.dtype)[None], (B, N_META, D))
    h = jnp.concatenate([pad, meta, x], axis=1)
    valid = jnp.arange(L) >= PAD_FRONT
    vmask = valid.astype(x.dtype)[None, :, None]
    slopes = 2.0 ** (-8.0 * jnp.arange(1, A_HEADS + 1, dtype=jnp.float32) / A_HEADS)
    log_g = jnp.log1p(-(2.0 ** (-5.0 - jnp.arange(R_HEADS, dtype=jnp.float32))))

    for l in range(DEPTH):
        lam_init = 0.8 - 0.6 * math.exp(-0.3 * l)
        u = rmsnorm(h, attn_norm[l]) * vmask
        proj = u @ w_in[l]
        aq, ak, av, rq, rk, rv, rg = jnp.split(proj, SPLITS, axis=-1)
        lq = lambda_qk[l].astype(jnp.float32)
        lam = jnp.exp(jnp.sum(lq[0] * lq[1])) - jnp.exp(jnp.sum(lq[2] * lq[3])) + lam_init
        ya = diff_attention(aq.reshape(B, L, A_HEADS, 2, A_QK_DIM),
                            ak.reshape(B, L, A_HEADS, 2, A_QK_DIM),
                            av.reshape(B, L, A_HEADS, A_V_DIM), lam, slopes, valid)
        ya = rmsnorm(ya, attn_subln[l]) * (1.0 - lam_init)
        yr = retention(rq.reshape(B, L, R_HEADS, R_QK_DIM),
                       rk.reshape(B, L, R_HEADS, R_QK_DIM),
                       rv.reshape(B, L, R_HEADS, R_V_DIM), log_g)
        yr = rmsnorm(yr, ret_norm[l]) * jax.nn.silu(rg.reshape(B, L, R_HEADS, R_V_DIM))
        y = jnp.concatenate([ya.reshape(B, L, D_A), yr.reshape(B, L, D_R)], axis=-1)
        h = h + y @ w_out[l]
        u = rmsnorm(h, ffn_norm[l]) * vmask
        h = h + conv_glu(u, w_gate[l], w_up[l], conv_w[l], w_down[l])

    out = rmsnorm(h, final_norm)
    return out[:, CHUNK:]
```

```python
import functools
import math

import jax
import jax.numpy as jnp
from jax import lax
from jax.experimental import pallas as pl
from jax.experimental.pallas import tpu as pltpu

N_META = 16
CHUNK = 128
PAD_FRONT = CHUNK - N_META
A_QK_DIM = 64
A_V_DIM = 128
R_QK_DIM = 128
R_V_DIM = 256
CONV_W = 3
EPS = 1e-6
NEG_INF = -1e30
CONV_HALO = 8

V7X_VMEM_BYTES = 64 * 1024 * 1024
VMEM_LIMIT_BYTES = V7X_VMEM_BYTES - 8 * 1024 * 1024

F32 = jnp.float32
BF16 = jnp.bfloat16


def _pick(n, candidates):
    for c in candidates:
        if n % c == 0:
            return c
    raise ValueError(f"no tile in {candidates} divides {n}")


def _params(*semantics):
    return pltpu.CompilerParams(dimension_semantics=semantics,
                                vmem_limit_bytes=VMEM_LIMIT_BYTES)


def _rms(xf, g):
    return xf * lax.rsqrt(jnp.mean(xf * xf, axis=-1, keepdims=True) + EPS) * g


def _masked_norm(hn, g, seq_row0):
    row = lax.broadcasted_iota(jnp.int32, hn.shape, 0) + seq_row0
    return jnp.where(row >= PAD_FRONT, _rms(hn, g), 0.0)


def _prep_kernel(x_ref, meta_ref, g_ref, h_ref, u_ref):
    i = pl.program_id(1)
    g = g_ref[...]

    @pl.when(i == 0)
    def _():
        meta = meta_ref[...]
        h_ref[0, :PAD_FRONT, :] = jnp.zeros((PAD_FRONT, meta.shape[1]), F32)
        h_ref[0, PAD_FRONT:, :] = meta
        u_ref[0, :PAD_FRONT, :] = jnp.zeros((PAD_FRONT, meta.shape[1]), BF16)
        u_ref[0, PAD_FRONT:, :] = _rms(meta, g).astype(BF16)

    @pl.when(i > 0)
    def _():
        x = x_ref[0]
        h_ref[0] = x
        u_ref[0] = _rms(x, g).astype(BF16)


def _prep(x, meta, g):
    B, S, D = x.shape
    L = CHUNK + S
    return pl.pallas_call(
        _prep_kernel,
        grid=(B, L // CHUNK),
        in_specs=[
            pl.BlockSpec((1, CHUNK, D), lambda b, i: (b, jnp.maximum(i - 1, 0), 0)),
            pl.BlockSpec((N_META, D), lambda b, i: (0, 0)),
            pl.BlockSpec((1, D), lambda b, i: (0, 0)),
        ],
        out_specs=[
            pl.BlockSpec((1, CHUNK, D), lambda b, i: (b, i, 0)),
            pl.BlockSpec((1, CHUNK, D), lambda b, i: (b, i, 0)),
        ],
        out_shape=[jax.ShapeDtypeStruct((B, L, D), F32),
                   jax.ShapeDtypeStruct((B, L, D), BF16)],
        compiler_params=_params("parallel", "arbitrary"),
    )(x, meta, g)


def _matmul_kernel(x_ref, w_ref, o_ref):
    o_ref[...] = jnp.dot(x_ref[...], w_ref[...],
                         preferred_element_type=F32).astype(o_ref.dtype)


def _in_proj(u, w):
    R, D = u.shape
    N = w.shape[1]
    tm = _pick(R, (1408, 1024, 768, 512, 384, 256, 128))
    tn = _pick(N, (1024, 512, 256, 128))
    return pl.pallas_call(
        _matmul_kernel,
        grid=(R // tm, N // tn),
        in_specs=[pl.BlockSpec((tm, D), lambda m, n: (m, 0)),
                  pl.BlockSpec((D, tn), lambda m, n: (0, n))],
        out_specs=pl.BlockSpec((tm, tn), lambda m, n: (m, n)),
        out_shape=jax.ShapeDtypeStruct((R, N), BF16),
        compiler_params=_params("parallel", "arbitrary"),
    )(u, w)


def _attn_kernel(slopes_ref, lq_ref, subln_ref, q_ref, k_ref, v_ref, o_ref,
                 acc_ref, m_ref, l_ref, *, tq, lam_init):
    h = pl.program_id(1)
    qi = pl.program_id(2)
    slope = slopes_ref[h]

    q = q_ref[0]
    lane = lax.broadcasted_iota(jnp.int32, q.shape, 1)
    scale = A_QK_DIM ** -0.5
    zero = jnp.zeros_like(q)
    q_maps = (jnp.where(lane < A_QK_DIM, q, zero) * scale,
              jnp.where(lane >= A_QK_DIM, q, zero) * scale)

    m_ref[...] = jnp.full(m_ref.shape, NEG_INF, F32)
    l_ref[...] = jnp.zeros(l_ref.shape, F32)
    acc_ref[...] = jnp.zeros(acc_ref.shape, F32)

    def body(j, carry):
        k0 = pl.multiple_of(j * tq, tq)
        k = k_ref[0, pl.ds(k0, tq), :]
        v = v_ref[0, pl.ds(k0, tq), :]
        row = lax.broadcasted_iota(jnp.int32, (tq, tq), 0) + qi * tq
        col = lax.broadcasted_iota(jnp.int32, (tq, tq), 1) + k0
        dist = row - col
        mask = (dist >= 0) & (col >= PAD_FRONT)
        bias = dist.astype(F32) * (-slope)
        for mi in range(2):
            s = lax.dot_general(q_maps[mi], k, (((1,), (1,)), ((), ())),
                                preferred_element_type=F32)
            s = jnp.where(mask, s + bias, NEG_INF)
            m_old = m_ref[mi]
            m_new = jnp.maximum(m_old, jnp.max(s, axis=-1, keepdims=True))
            alpha = jnp.exp(m_old - m_new)
            p = jnp.exp(s - m_new)
            l_ref[mi] = alpha * l_ref[mi] + jnp.sum(p, axis=-1, keepdims=True)
            acc_ref[mi] = alpha * acc_ref[mi] + jnp.dot(
                p.astype(BF16), v, preferred_element_type=F32)
            m_ref[mi] = m_new
        return carry

    lax.fori_loop(0, qi + 1, body, 0)

    lq = lq_ref[...]
    lam = (jnp.exp(jnp.sum(lq[0:1] * lq[1:2], keepdims=True))
           - jnp.exp(jnp.sum(lq[2:3] * lq[3:4], keepdims=True)) + lam_init)
    a = acc_ref[0] / l_ref[0] - lam * (acc_ref[1] / l_ref[1])
    y = _rms(a, subln_ref[...]) * (1.0 - lam_init)
    o_ref[0] = y.astype(o_ref.dtype)


def _attention(proj, slopes, lq, subln, lam_init, n_heads):
    B, L, _ = proj.shape
    tq = _pick(L, (384, 128))
    hd = A_V_DIM
    kern = functools.partial(_attn_kernel, tq=tq, lam_init=lam_init)
    return pl.pallas_call(
        kern,
        grid=(B, n_heads, L // tq),
        in_specs=[
            pl.BlockSpec(memory_space=pltpu.SMEM),
            pl.BlockSpec((4, A_QK_DIM), lambda b, h, i: (0, 0)),
            pl.BlockSpec((1, hd), lambda b, h, i: (0, 0)),
            pl.BlockSpec((1, tq, hd), lambda b, h, i: (b, i, h)),
            pl.BlockSpec((1, L, hd), lambda b, h, i: (b, 0, n_heads + h)),
            pl.BlockSpec((1, L, hd), lambda b, h, i: (b, 0, 2 * n_heads + h)),
        ],
        out_specs=pl.BlockSpec((1, tq, hd), lambda b, h, i: (b, i, h)),
        out_shape=jax.ShapeDtypeStruct((B, L, n_heads * hd), BF16),
        scratch_shapes=[pltpu.VMEM((2, tq, hd), F32),
                        pltpu.VMEM((2, tq, 1), F32),
                        pltpu.VMEM((2, tq, 1), F32)],
        compiler_params=_params("parallel", "parallel", "arbitrary"),
    )(slopes, lq, subln, proj, proj, proj)


def _ret_kernel(logg_ref, rn_ref, q_ref, k_ref, v_ref, g_ref, o_ref, s_ref,
                *, n_heads, n_chunks):
    @pl.when(pl.program_id(1) == 0)
    def _():
        s_ref[...] = jnp.zeros(s_ref.shape, F32)

    C = CHUNK
    ri = lax.broadcasted_iota(jnp.int32, (C, C), 0)
    ci = lax.broadcasted_iota(jnp.int32, (C, C), 1)
    diff = (ri - ci).astype(F32)
    idx = lax.broadcasted_iota(jnp.int32, (C, 1), 0).astype(F32)
    scale = R_QK_DIM ** -0.5
    rn = rn_ref[...]

    for h in range(n_heads):
        lg = logg_ref[h]
        decay_in = jnp.where(diff >= 0, jnp.exp(lg * jnp.maximum(diff, 0.0)), 0.0) * scale
        k_decay = jnp.exp(lg * (C - 1.0 - idx)) * scale
        q_decay = jnp.exp(lg * (idx + 1.0))
        g_chunk = jnp.exp(jnp.full((1, 1), lg * C, F32))
        qk = slice(h * R_QK_DIM, (h + 1) * R_QK_DIM)
        vv = slice(h * R_V_DIM, (h + 1) * R_V_DIM)
        for c in range(n_chunks):
            rows = slice(c * C, (c + 1) * C)
            q = q_ref[0, rows, qk]
            k = k_ref[0, rows, qk]
            v = v_ref[0, rows, vv]
            s = lax.dot_general(q, k, (((1,), (1,)), ((), ())),
                                preferred_element_type=F32) * decay_in
            intra = jnp.dot(s.astype(BF16), v, preferred_element_type=F32)
            s_prev = s_ref[h]
            q_dec = (q.astype(F32) * q_decay).astype(BF16)
            cross = jnp.dot(q_dec, s_prev.astype(BF16), preferred_element_type=F32)
            k_dec = (k.astype(F32) * k_decay).astype(BF16)
            kv = lax.dot_general(k_dec, v, (((0,), (0,)), ((), ())),
                                 preferred_element_type=F32)
            s_ref[h] = g_chunk * s_prev + kv
            y = _rms(intra + cross, rn)
            gate = g_ref[0, rows, vv].astype(F32)
            o_ref[0, rows, vv] = (y * (gate * jax.nn.sigmoid(gate))).astype(o_ref.dtype)


def _retention(proj, log_g, rn, n_heads, col0):
    B, L, _ = proj.shape
    dqk = n_heads * R_QK_DIM
    dv = n_heads * R_V_DIM
    T = _pick(L, (384, 128))
    kern = functools.partial(_ret_kernel, n_heads=n_heads, n_chunks=T // CHUNK)
    q_blk = col0 // dqk
    v_blk = (col0 + 2 * dqk) // dv
    return pl.pallas_call(
        kern,
        grid=(B, L // T),
        in_specs=[
            pl.BlockSpec(memory_space=pltpu.SMEM),
            pl.BlockSpec((1, R_V_DIM), lambda b, n: (0, 0)),
            pl.BlockSpec((1, T, dqk), lambda b, n: (b, n, q_blk)),
            pl.BlockSpec((1, T, dqk), lambda b, n: (b, n, q_blk + 1)),
            pl.BlockSpec((1, T, dv), lambda b, n: (b, n, v_blk)),
            pl.BlockSpec((1, T, dv), lambda b, n: (b, n, v_blk + 1)),
        ],
        out_specs=pl.BlockSpec((1, T, dv), lambda b, n: (b, n, 0)),
        out_shape=jax.ShapeDtypeStruct((B, L, dv), BF16),
        scratch_shapes=[pltpu.VMEM((n_heads, R_QK_DIM, R_V_DIM), F32)],
        compiler_params=_params("parallel", "arbitrary"),
    )(log_g, rn, proj, proj, proj, proj)


def _out_proj_kernel(ya_ref, yr_ref, w_ref, h_ref, g_ref, ho_ref, u_ref,
                     *, tm, tiles_per_seq):
    da = ya_ref.shape[1]
    acc = jnp.dot(ya_ref[...], w_ref[:da, :], preferred_element_type=F32)
    acc = acc + jnp.dot(yr_ref[...], w_ref[da:, :], preferred_element_type=F32)
    hn = h_ref[...] + acc
    ho_ref[...] = hn
    seq_row0 = (pl.program_id(0) % tiles_per_seq) * tm
    u_ref[...] = _masked_norm(hn, g_ref[...], seq_row0).astype(u_ref.dtype)


def _out_proj(ya, yr, w, h, g, L):
    R, D = h.shape
    da, dr = ya.shape[1], yr.shape[1]
    tm = _pick(L, (528, 384, 128))
    kern = functools.partial(_out_proj_kernel, tm=tm, tiles_per_seq=L // tm)
    return pl.pallas_call(
        kern,
        grid=(R // tm,),
        in_specs=[
            pl.BlockSpec((tm, da), lambda m: (m, 0)),
            pl.BlockSpec((tm, dr), lambda m: (m, 0)),
            pl.BlockSpec((da + dr, D), lambda m: (0, 0)),
            pl.BlockSpec((tm, D), lambda m: (m, 0)),
            pl.BlockSpec((1, D), lambda m: (0, 0)),
        ],
        out_specs=[pl.BlockSpec((tm, D), lambda m: (m, 0)),
                   pl.BlockSpec((tm, D), lambda m: (m, 0))],
        out_shape=[jax.ShapeDtypeStruct((R, D), F32),
                   jax.ShapeDtypeStruct((R, D), BF16)],
        input_output_aliases={3: 0},
        compiler_params=_params("parallel"),
    )(ya, yr, w, h, g)


def _ffn_kernel(u_ref, wg_ref, wu_ref, cw_ref, wd_ref, h_ref, g_ref, *rest,
                tm, tiles_per_seq, final):
    if final:
        ho_ref, gbuf_ref, carry_ref = rest
    else:
        ho_ref, u_out_ref, gbuf_ref, carry_ref = rest
    m = pl.program_id(0)
    f = pl.program_id(1)
    seq_tile = m % tiles_per_seq

    u = u_ref[...]
    gate = jnp.dot(u, wg_ref[...], preferred_element_type=F32)
    up = jnp.dot(u, wu_ref[...], preferred_element_type=F32)

    @pl.when(seq_tile == 0)
    def _():
        gbuf_ref[:CONV_HALO, :] = jnp.zeros((CONV_HALO, gate.shape[1]), F32)

    @pl.when(seq_tile != 0)
    def _():
        gbuf_ref[:CONV_HALO, :] = carry_ref[f]

    gbuf_ref[CONV_HALO:, :] = gate
    carry_ref[f] = gate[tm - CONV_HALO:, :]

    cw = cw_ref[...]
    gc = gate * cw[CONV_W - 1:CONV_W, :]
    for i in range(CONV_W - 1):
        back = CONV_W - 1 - i
        gc = gc + gbuf_ref[pl.ds(CONV_HALO - back, tm), :] * cw[i:i + 1, :]
    act = (gc * jax.nn.sigmoid(gc)) * up
    contrib = jnp.dot(act.astype(BF16), wd_ref[...], preferred_element_type=F32)

    @pl.when(f == 0)
    def _():
        ho_ref[...] = h_ref[...] + contrib

    @pl.when(f > 0)
    def _():
        ho_ref[...] += contrib

    @pl.when(f == pl.num_programs(1) - 1)
    def _():
        hn = ho_ref[...]
        g = g_ref[...]
        if final:
            ho_ref[...] = _rms(hn, g)
        else:
            u_out_ref[...] = _masked_norm(hn, g, seq_tile * tm).astype(u_out_ref.dtype)


def _ffn(u, wg, wu, cw, wd, h, g, L, final):
    R, D = h.shape
    F = wg.shape[1]
    tm = _pick(L, (704, 384, 128))
    tf = _pick(F, (512, 256, 128))
    kern = functools.partial(_ffn_kernel, tm=tm, tiles_per_seq=L // tm, final=final)
    row_spec = pl.BlockSpec((tm, D), lambda m, f: (m, 0))
    out_specs = [row_spec]
    out_shape = [jax.ShapeDtypeStruct((R, D), F32)]
    if not final:
        out_specs.append(row_spec)
        out_shape.append(jax.ShapeDtypeStruct((R, D), BF16))
    return pl.pallas_call(
        kern,
        grid=(R // tm, F // tf),
        in_specs=[
            row_spec,
            pl.BlockSpec((D, tf), lambda m, f: (0, f)),
            pl.BlockSpec((D, tf), lambda m, f: (0, f)),
            pl.BlockSpec((CONV_W, tf), lambda m, f: (0, f)),
            pl.BlockSpec((tf, D), lambda m, f: (f, 0)),
            row_spec,
            pl.BlockSpec((1, D), lambda m, f: (0, 0)),
        ],
        out_specs=out_specs,
        out_shape=out_shape,
        scratch_shapes=[pltpu.VMEM((tm + CONV_HALO, tf), F32),
                        pltpu.VMEM((F // tf, CONV_HALO, tf), F32)],
        input_output_aliases={5: 0},
        compiler_params=_params("arbitrary", "arbitrary"),
    )(u, wg, wu, cw, wd, h, g)


def kernel(x, meta_tokens, attn_norm, w_in, lambda_qk, attn_subln, ret_norm, w_out,
           ffn_norm, w_gate, w_up, conv_w, w_down, final_norm):
    B, S, D = x.shape
    L = CHUNK + S
    R = B * L
    depth = w_in.shape[0]
    a_heads = D // 256
    r_heads = D // 512
    d_a = a_heads * A_V_DIM
    ret_col0 = 2 * a_heads * 2 * A_QK_DIM + d_a

    slopes = 2.0 ** (-8.0 * jnp.arange(1, a_heads + 1, dtype=F32) / a_heads)
    log_g = jnp.log1p(-(2.0 ** (-5.0 - jnp.arange(r_heads, dtype=F32))))

    w_in_b = w_in.astype(BF16)
    w_out_b = w_out.astype(BF16)
    w_gate_b = w_gate.astype(BF16)
    w_up_b = w_up.astype(BF16)
    w_down_b = w_down.astype(BF16)

    h, u = _prep(x, meta_tokens, attn_norm[0:1])
    h = h.reshape(R, D)
    u = u.reshape(R, D)
    for l in range(depth):
        lam_init = 0.8 - 0.6 * math.exp(-0.3 * l)
        proj = _in_proj(u, w_in_b[l]).reshape(B, L, -1)
        ya = _attention(proj, slopes, lambda_qk[l], attn_subln[l:l + 1], lam_init, a_heads)
        yr = _retention(proj, log_g, ret_norm[l:l + 1], r_heads, ret_col0)
        h, u2 = _out_proj(ya.reshape(R, -1), yr.reshape(R, -1), w_out_b[l], h,
                          ffn_norm[l:l + 1], L)
        final = l == depth - 1
        g_next = final_norm[None, :] if final else attn_norm[l + 1:l + 2]
        res = _ffn(u2, w_gate_b[l], w_up_b[l], conv_w[l], w_down_b[l], h, g_next, L, final)
        if final:
            h = res[0]
        else:
            h, u = res
    return h.reshape(B, L, D)[:, CHUNK:]
```

```python
import functools
import math

import jax
import jax.numpy as jnp
from jax import lax
from jax.experimental import pallas as pl
from jax.experimental.pallas import tpu as pltpu

N_META = 16
CHUNK = 128
PAD_FRONT = CHUNK - N_META
A_QK_DIM = 64
A_V_DIM = 128
R_QK_DIM = 128
R_V_DIM = 256
CONV_W = 3
EPS = 1e-6
NEG_INF = -1e30
CONV_HALO = 8

V7X_VMEM_BYTES = 64 * 1024 * 1024
VMEM_LIMIT_BYTES = V7X_VMEM_BYTES - 8 * 1024 * 1024

F32 = jnp.float32
BF16 = jnp.bfloat16


def _pick(n, candidates):
    for c in candidates:
        if n % c == 0:
            return c
    raise ValueError(f"no tile in {candidates} divides {n}")


def _params(*semantics):
    return pltpu.CompilerParams(dimension_semantics=semantics,
                                vmem_limit_bytes=VMEM_LIMIT_BYTES)


def _rms(xf, g):
    return xf * lax.rsqrt(jnp.mean(xf * xf, axis=-1, keepdims=True) + EPS) * g


def _masked_norm(hn, g, seq_row0):
    row = lax.broadcasted_iota(jnp.int32, hn.shape, 0) + seq_row0
    return jnp.where(row >= PAD_FRONT, _rms(hn, g), 0.0)


def _prep_kernel(x_ref, meta_ref, g_ref, h_ref, u_ref):
    i = pl.program_id(1)
    g = g_ref[...]

    @pl.when(i == 0)
    def _():
        meta = meta_ref[...]
        h_ref[0, :PAD_FRONT, :] = jnp.zeros((PAD_FRONT, meta.shape[1]), F32)
        h_ref[0, PAD_FRONT:, :] = meta
        u_ref[0, :PAD_FRONT, :] = jnp.zeros((PAD_FRONT, meta.shape[1]), BF16)
        u_ref[0, PAD_FRONT:, :] = _rms(meta, g).astype(BF16)

    @pl.when(i > 0)
    def _():
        x = x_ref[0]
        h_ref[0] = x
        u_ref[0] = _rms(x, g).astype(BF16)


def _prep(x, meta, g):
    B, S, D = x.shape
    L = CHUNK + S
    return pl.pallas_call(
        _prep_kernel,
        grid=(B, L // CHUNK),
        in_specs=[
            pl.BlockSpec((1, CHUNK, D), lambda b, i: (b, jnp.maximum(i - 1, 0), 0)),
            pl.BlockSpec((N_META, D), lambda b, i: (0, 0)),
            pl.BlockSpec((1, D), lambda b, i: (0, 0)),
        ],
        out_specs=[
            pl.BlockSpec((1, CHUNK, D), lambda b, i: (b, i, 0)),
            pl.BlockSpec((1, CHUNK, D), lambda b, i: (b, i, 0)),
        ],
        out_shape=[jax.ShapeDtypeStruct((B, L, D), F32),
                   jax.ShapeDtypeStruct((B, L, D), BF16)],
        compiler_params=_params("parallel", "arbitrary"),
    )(x, meta, g)


def _matmul_kernel(x_ref, w_ref, cs_ref, o_ref):
    acc = jnp.dot(x_ref[...], w_ref[...], preferred_element_type=F32)
    o_ref[...] = (acc * cs_ref[...]).astype(o_ref.dtype)


def _in_proj(u, w, col_scale):
    R, D = u.shape
    N = w.shape[1]
    tm = _pick(R, (1408, 1024, 768, 512, 384, 256, 128))
    tn = _pick(N, (1024, 512, 256, 128))
    return pl.pallas_call(
        _matmul_kernel,
        grid=(R // tm, N // tn),
        in_specs=[pl.BlockSpec((tm, D), lambda m, n: (m, 0)),
                  pl.BlockSpec((D, tn), lambda m, n: (0, n)),
                  pl.BlockSpec((1, tn), lambda m, n: (0, n))],
        out_specs=pl.BlockSpec((tm, tn), lambda m, n: (m, n)),
        out_shape=jax.ShapeDtypeStruct((R, N), BF16),
        compiler_params=_params("parallel", "arbitrary"),
    )(u, w, col_scale)


ATTN_TQ = 128
ATTN_TK = 256
ATTN_HEAD_GROUP = 4
LOG2E = math.log2(math.e)


def _attn_kernel(slopes_ref, lq_ref, subln_ref, q_ref, k_ref, v_ref, o_ref,
                 q2_ref, acc_ref, l_ref, m_ref, *, n_heads, lam_init):
    tq, tk = ATTN_TQ, ATTN_TK
    hd = A_V_DIM
    qi = pl.program_id(1)

    lane = lax.broadcasted_iota(jnp.int32, (tq, hd), 1)
    for g in range(n_heads):
        q = q_ref[0, :, g * hd:(g + 1) * hd]
        zero = jnp.zeros_like(q)
        q2_ref[g, :tq, :] = jnp.where(lane < A_QK_DIM, q, zero)
        q2_ref[g, tq:, :] = jnp.where(lane >= A_QK_DIM, q, zero)

    m_ref[...] = jnp.full(m_ref.shape, NEG_INF, F32)
    l_ref[...] = jnp.zeros(l_ref.shape, F32)
    acc_ref[...] = jnp.zeros(acc_ref.shape, F32)

    def tile(k0, first_key):
        col = lax.broadcasted_iota(jnp.int32, (1, tk), 1) + k0
        colf = col.astype(F32)
        if first_key is not None:
            row = lax.broadcasted_iota(jnp.int32, (2 * tq, tk), 0)
            qpos = jnp.where(row >= tq, row - tq, row) + qi * tq
            kpos = lax.broadcasted_iota(jnp.int32, (2 * tq, tk), 1) + k0
            visible = (kpos >= first_key) & (kpos <= qpos)
        for g0 in range(0, n_heads, ATTN_HEAD_GROUP):
            head_group(range(g0, min(g0 + ATTN_HEAD_GROUP, n_heads)), k0, first_key, col, colf,
                       visible if first_key is not None else None)

    def head_group(heads, k0, first_key, col, colf, visible):
        cols = {g: slice(g * hd, (g + 1) * hd) for g in heads}
        scores = {g: lax.dot_general(q2_ref[g], k_ref[0, pl.ds(k0, tk), cols[g]],
                                     (((1,), (1,)), ((), ())), preferred_element_type=F32)
                  for g in heads}
        probs, alphas = {}, {}
        for g in heads:
            bias = jnp.where(col >= PAD_FRONT, colf * (slopes_ref[g] * LOG2E), NEG_INF)
            s = scores[g] + bias
            if first_key is not None:
                s = jnp.where(visible, s, NEG_INF)
            m_old = m_ref[g]
            m_new = jnp.maximum(m_old, jnp.max(s, axis=-1, keepdims=True))
            alpha = jnp.exp2(m_old - m_new)
            p_parts = [jnp.exp2(s[:, c:c + hd] - m_new) for c in range(0, tk, hd)]
            l_ref[g] = alpha * l_ref[g] + sum(p_parts[1:], p_parts[0])
            m_ref[g] = m_new
            probs[g] = jnp.concatenate(p_parts, axis=1).astype(BF16)
            alphas[g] = alpha
        for g in heads:
            acc_ref[g] = alphas[g] * acc_ref[g] + jnp.dot(
                probs[g], v_ref[0, pl.ds(k0, tk), cols[g]], preferred_element_type=F32)

    n_full = (qi * tq) // tk

    def body(j, carry):
        tile(pl.multiple_of(j * tk, tk), None)
        return carry

    lax.fori_loop(0, n_full, body, 0)
    tail0 = pl.multiple_of(jnp.maximum(qi * tq + tq - tk, 0), tq)
    tile(tail0, n_full * tk)

    lq = lq_ref[...]
    lam = (jnp.exp(jnp.sum(lq[0:1] * lq[1:2], keepdims=True))
           - jnp.exp(jnp.sum(lq[2:3] * lq[3:4], keepdims=True)) + lam_init)
    subln = subln_ref[...]
    for g in range(n_heads):
        out = acc_ref[g] / jnp.sum(l_ref[g], axis=-1, keepdims=True)
        a = out[:tq] - lam * out[tq:]
        y = _rms(a, subln) * (1.0 - lam_init)
        o_ref[0, :, g * hd:(g + 1) * hd] = y.astype(o_ref.dtype)


def _attention(proj, slopes, lq, subln, lam_init, n_heads):
    B, L, _ = proj.shape
    tq = ATTN_TQ
    hd = A_V_DIM
    d_a = n_heads * hd
    kern = functools.partial(_attn_kernel, n_heads=n_heads, lam_init=lam_init)
    return pl.pallas_call(
        kern,
        grid=(B, L // tq),
        in_specs=[
            pl.BlockSpec(memory_space=pltpu.SMEM),
            pl.BlockSpec((4, A_QK_DIM), lambda b, i: (0, 0)),
            pl.BlockSpec((1, hd), lambda b, i: (0, 0)),
            pl.BlockSpec((1, tq, d_a), lambda b, i: (b, i, 0)),
            pl.BlockSpec((1, L, d_a), lambda b, i: (b, 0, 1)),
            pl.BlockSpec((1, L, d_a), lambda b, i: (b, 0, 2)),
        ],
        out_specs=pl.BlockSpec((1, tq, d_a), lambda b, i: (b, i, 0)),
        out_shape=jax.ShapeDtypeStruct((B, L, d_a), BF16),
        scratch_shapes=[pltpu.VMEM((n_heads, 2 * tq, hd), BF16),
                        pltpu.VMEM((n_heads, 2 * tq, hd), F32),
                        pltpu.VMEM((n_heads, 2 * tq, hd), F32),
                        pltpu.VMEM((n_heads, 2 * tq, hd), F32)],
        compiler_params=_params("parallel", "arbitrary"),
    )(slopes, lq, subln, proj, proj, proj)


def _ret_kernel(logg_ref, rn_ref, q_ref, k_ref, v_ref, g_ref, o_ref, s_ref,
                *, n_heads, n_chunks):
    @pl.when(pl.program_id(1) == 0)
    def _():
        s_ref[...] = jnp.zeros(s_ref.shape, F32)

    C = CHUNK
    ri = lax.broadcasted_iota(jnp.int32, (C, C), 0)
    ci = lax.broadcasted_iota(jnp.int32, (C, C), 1)
    diff = (ri - ci).astype(F32)
    idx = lax.broadcasted_iota(jnp.int32, (C, 1), 0).astype(F32)
    scale = R_QK_DIM ** -0.5
    rn = rn_ref[...]

    for h in range(n_heads):
        lg = logg_ref[h]
        decay_in = jnp.where(diff >= 0, jnp.exp(lg * jnp.maximum(diff, 0.0)), 0.0) * scale
        k_decay = jnp.exp(lg * (C - 1.0 - idx)) * scale
        q_decay = jnp.exp(lg * (idx + 1.0))
        g_chunk = jnp.exp(jnp.full((1, 1), lg * C, F32))
        qk = slice(h * R_QK_DIM, (h + 1) * R_QK_DIM)
        vv = slice(h * R_V_DIM, (h + 1) * R_V_DIM)
        for c in range(n_chunks):
            rows = slice(c * C, (c + 1) * C)
            q = q_ref[0, rows, qk]
            k = k_ref[0, rows, qk]
            v = v_ref[0, rows, vv]
            s = lax.dot_general(q, k, (((1,), (1,)), ((), ())),
                                preferred_element_type=F32) * decay_in
            intra = jnp.dot(s.astype(BF16), v, preferred_element_type=F32)
            s_prev = s_ref[h]
            q_dec = (q.astype(F32) * q_decay).astype(BF16)
            cross = jnp.dot(q_dec, s_prev.astype(BF16), preferred_element_type=F32)
            k_dec = (k.astype(F32) * k_decay).astype(BF16)
            kv = lax.dot_general(k_dec, v, (((0,), (0,)), ((), ())),
                                 preferred_element_type=F32)
            s_ref[h] = g_chunk * s_prev + kv
            y = _rms(intra + cross, rn)
            gate = g_ref[0, rows, vv].astype(F32)
            o_ref[0, rows, vv] = (y * (gate * jax.nn.sigmoid(gate))).astype(o_ref.dtype)


def _retention(proj, log_g, rn, n_heads, col0):
    B, L, _ = proj.shape
    dqk = n_heads * R_QK_DIM
    dv = n_heads * R_V_DIM
    T = _pick(L, (384, 128))
    kern = functools.partial(_ret_kernel, n_heads=n_heads, n_chunks=T // CHUNK)
    q_blk = col0 // dqk
    v_blk = (col0 + 2 * dqk) // dv
    return pl.pallas_call(
        kern,
        grid=(B, L // T),
        in_specs=[
            pl.BlockSpec(memory_space=pltpu.SMEM),
            pl.BlockSpec((1, R_V_DIM), lambda b, n: (0, 0)),
            pl.BlockSpec((1, T, dqk), lambda b, n: (b, n, q_blk)),
            pl.BlockSpec((1, T, dqk), lambda b, n: (b, n, q_blk + 1)),
            pl.BlockSpec((1, T, dv), lambda b, n: (b, n, v_blk)),
            pl.BlockSpec((1, T, dv), lambda b, n: (b, n, v_blk + 1)),
        ],
        out_specs=pl.BlockSpec((1, T, dv), lambda b, n: (b, n, 0)),
        out_shape=jax.ShapeDtypeStruct((B, L, dv), BF16),
        scratch_shapes=[pltpu.VMEM((n_heads, R_QK_DIM, R_V_DIM), F32)],
        compiler_params=_params("parallel", "arbitrary"),
    )(log_g, rn, proj, proj, proj, proj)


def _out_proj_kernel(ya_ref, yr_ref, w_ref, h_ref, g_ref, ho_ref, u_ref,
                     *, tm, tiles_per_seq):
    da = ya_ref.shape[1]
    acc = jnp.dot(ya_ref[...], w_ref[:da, :], preferred_element_type=F32)
    acc = acc + jnp.dot(yr_ref[...], w_ref[da:, :], preferred_element_type=F32)
    hn = h_ref[...] + acc
    ho_ref[...] = hn
    seq_row0 = (pl.program_id(0) % tiles_per_seq) * tm
    u_ref[...] = _masked_norm(hn, g_ref[...], seq_row0).astype(u_ref.dtype)


def _out_proj(ya, yr, w, h, g, L):
    R, D = h.shape
    da, dr = ya.shape[1], yr.shape[1]
    tm = _pick(L, (528, 384, 128))
    kern = functools.partial(_out_proj_kernel, tm=tm, tiles_per_seq=L // tm)
    return pl.pallas_call(
        kern,
        grid=(R // tm,),
        in_specs=[
            pl.BlockSpec((tm, da), lambda m: (m, 0)),
            pl.BlockSpec((tm, dr), lambda m: (m, 0)),
            pl.BlockSpec((da + dr, D), lambda m: (0, 0)),
            pl.BlockSpec((tm, D), lambda m: (m, 0)),
            pl.BlockSpec((1, D), lambda m: (0, 0)),
        ],
        out_specs=[pl.BlockSpec((tm, D), lambda m: (m, 0)),
                   pl.BlockSpec((tm, D), lambda m: (m, 0))],
        out_shape=[jax.ShapeDtypeStruct((R, D), F32),
                   jax.ShapeDtypeStruct((R, D), BF16)],
        input_output_aliases={3: 0},
        compiler_params=_params("parallel"),
    )(ya, yr, w, h, g)


def _ffn_kernel(u_ref, wg_ref, wu_ref, cw_ref, wd_ref, h_ref, g_ref, *rest,
                tm, tiles_per_seq, final):
    if final:
        ho_ref, gbuf_ref, carry_ref = rest
    else:
        ho_ref, u_out_ref, gbuf_ref, carry_ref = rest
    m = pl.program_id(0)
    f = pl.program_id(1)
    seq_tile = m % tiles_per_seq

    u = u_ref[...]
    gate = jnp.dot(u, wg_ref[...], preferred_element_type=F32)
    up = jnp.dot(u, wu_ref[...], preferred_element_type=F32)

    @pl.when(seq_tile == 0)
    def _():
        gbuf_ref[:CONV_HALO, :] = jnp.zeros((CONV_HALO, gate.shape[1]), F32)

    @pl.when(seq_tile != 0)
    def _():
        gbuf_ref[:CONV_HALO, :] = carry_ref[f]

    gbuf_ref[CONV_HALO:, :] = gate
    carry_ref[f] = gate[tm - CONV_HALO:, :]

    cw = cw_ref[...]
    gc = gate * cw[CONV_W - 1:CONV_W, :]
    for i in range(CONV_W - 1):
        back = CONV_W - 1 - i
        gc = gc + gbuf_ref[pl.ds(CONV_HALO - back, tm), :] * cw[i:i + 1, :]
    act = (gc * jax.nn.sigmoid(gc)) * up
    contrib = jnp.dot(act.astype(BF16), wd_ref[...], preferred_element_type=F32)

    @pl.when(f == 0)
    def _():
        ho_ref[...] = h_ref[...] + contrib

    @pl.when(f > 0)
    def _():
        ho_ref[...] += contrib

    @pl.when(f == pl.num_programs(1) - 1)
    def _():
        hn = ho_ref[...]
        g = g_ref[...]
        if final:
            ho_ref[...] = _rms(hn, g)
        else:
            u_out_ref[...] = _masked_norm(hn, g, seq_tile * tm).astype(u_out_ref.dtype)


def _ffn(u, wg, wu, cw, wd, h, g, L, final):
    R, D = h.shape
    F = wg.shape[1]
    tm = _pick(L, (704, 384, 128))
    tf = _pick(F, (512, 256, 128))
    kern = functools.partial(_ffn_kernel, tm=tm, tiles_per_seq=L // tm, final=final)
    row_spec = pl.BlockSpec((tm, D), lambda m, f: (m, 0))
    out_specs = [row_spec]
    out_shape = [jax.ShapeDtypeStruct((R, D), F32)]
    if not final:
        out_specs.append(row_spec)
        out_shape.append(jax.ShapeDtypeStruct((R, D), BF16))
    return pl.pallas_call(
        kern,
        grid=(R // tm, F // tf),
        in_specs=[
            row_spec,
            pl.BlockSpec((D, tf), lambda m, f: (0, f)),
            pl.BlockSpec((D, tf), lambda m, f: (0, f)),
            pl.BlockSpec((CONV_W, tf), lambda m, f: (0, f)),
            pl.BlockSpec((tf, D), lambda m, f: (f, 0)),
            row_spec,
            pl.BlockSpec((1, D), lambda m, f: (0, 0)),
        ],
        out_specs=out_specs,
        out_shape=out_shape,
        scratch_shapes=[pltpu.VMEM((tm + CONV_HALO, tf), F32),
                        pltpu.VMEM((F // tf, CONV_HALO, tf), F32)],
        input_output_aliases={5: 0},
        compiler_params=_params("arbitrary", "arbitrary"),
    )(u, wg, wu, cw, wd, h, g)


def kernel(x, meta_tokens, attn_norm, w_in, lambda_qk, attn_subln, ret_norm, w_out,
           ffn_norm, w_gate, w_up, conv_w, w_down, final_norm):
    B, S, D = x.shape
    L = CHUNK + S
    R = B * L
    depth = w_in.shape[0]
    a_heads = D // 256
    r_heads = D // 512
    d_a = a_heads * A_V_DIM
    ret_col0 = 2 * a_heads * 2 * A_QK_DIM + d_a

    slopes = 2.0 ** (-8.0 * jnp.arange(1, a_heads + 1, dtype=F32) / a_heads)
    log_g = jnp.log1p(-(2.0 ** (-5.0 - jnp.arange(r_heads, dtype=F32))))

    n_aq = a_heads * 2 * A_QK_DIM
    col_scale = jnp.where(jnp.arange(w_in.shape[2]) < n_aq,
                          LOG2E * A_QK_DIM ** -0.5, 1.0).astype(F32)[None, :]

    w_in_b = w_in.astype(BF16)
    w_out_b = w_out.astype(BF16)
    w_gate_b = w_gate.astype(BF16)
    w_up_b = w_up.astype(BF16)
    w_down_b = w_down.astype(BF16)

    h, u = _prep(x, meta_tokens, attn_norm[0:1])
    h = h.reshape(R, D)
    u = u.reshape(R, D)
    for l in range(depth):
        lam_init = 0.8 - 0.6 * math.exp(-0.3 * l)
        proj = _in_proj(u, w_in_b[l], col_scale).reshape(B, L, -1)
        ya = _attention(proj, slopes, lambda_qk[l], attn_subln[l:l + 1], lam_init, a_heads)
        yr = _retention(proj, log_g, ret_norm[l:l + 1], r_heads, ret_col0)
        h, u2 = _out_proj(ya.reshape(R, -1), yr.reshape(R, -1), w_out_b[l], h,
                          ffn_norm[l:l + 1], L)
        final = l == depth - 1
        g_next = final_norm[None, :] if final else attn_norm[l + 1:l + 2]
        res = _ffn(u2, w_gate_b[l], w_up_b[l], conv_w[l], w_down_b[l], h, g_next, L, final)
        if final:
            h = res[0]
        else:
            h, u = res
    return h.reshape(B, L, D)[:, CHUNK:]
```

```python
import functools
import math

import jax
import jax.numpy as jnp
from jax import lax
from jax.experimental import pallas as pl
from jax.experimental.pallas import tpu as pltpu

N_META = 16
CHUNK = 128
PAD_FRONT = CHUNK - N_META
A_QK_DIM = 64
A_V_DIM = 128
R_QK_DIM = 128
R_V_DIM = 256
CONV_W = 3
EPS = 1e-6
NEG_INF = -1e30
CONV_HALO = 8
FFN_SUB = 256

V7X_VMEM_BYTES = 64 * 1024 * 1024
VMEM_LIMIT_BYTES = V7X_VMEM_BYTES - 8 * 1024 * 1024

F32 = jnp.float32
BF16 = jnp.bfloat16


def _pick(n, candidates):
    for c in candidates:
        if n % c == 0:
            return c
    raise ValueError(f"no tile in {candidates} divides {n}")


def _params(*semantics):
    return pltpu.CompilerParams(dimension_semantics=semantics,
                                vmem_limit_bytes=VMEM_LIMIT_BYTES)


def _rms(xf, g):
    return xf * lax.rsqrt(jnp.mean(xf * xf, axis=-1, keepdims=True) + EPS) * g


def _masked_norm(hn, g, seq_row0):
    row = lax.broadcasted_iota(jnp.int32, hn.shape, 0) + seq_row0
    return jnp.where(row >= PAD_FRONT, _rms(hn, g), 0.0)


def _prep_kernel(x_ref, meta_ref, g_ref, h_ref, u_ref):
    i = pl.program_id(1)
    g = g_ref[...]

    @pl.when(i == 0)
    def _():
        meta = meta_ref[...]
        h_ref[0, :PAD_FRONT, :] = jnp.zeros((PAD_FRONT, meta.shape[1]), F32)
        h_ref[0, PAD_FRONT:, :] = meta
        u_ref[0, :PAD_FRONT, :] = jnp.zeros((PAD_FRONT, meta.shape[1]), BF16)
        u_ref[0, PAD_FRONT:, :] = _rms(meta, g).astype(BF16)

    @pl.when(i > 0)
    def _():
        x = x_ref[0]
        h_ref[0] = x
        u_ref[0] = _rms(x, g).astype(BF16)


def _prep(x, meta, g):
    B, S, D = x.shape
    L = CHUNK + S
    return pl.pallas_call(
        _prep_kernel,
        grid=(B, L // CHUNK),
        in_specs=[
            pl.BlockSpec((1, CHUNK, D), lambda b, i: (b, jnp.maximum(i - 1, 0), 0)),
            pl.BlockSpec((N_META, D), lambda b, i: (0, 0)),
            pl.BlockSpec((1, D), lambda b, i: (0, 0)),
        ],
        out_specs=[
            pl.BlockSpec((1, CHUNK, D), lambda b, i: (b, i, 0)),
            pl.BlockSpec((1, CHUNK, D), lambda b, i: (b, i, 0)),
        ],
        out_shape=[jax.ShapeDtypeStruct((B, L, D), F32),
                   jax.ShapeDtypeStruct((B, L, D), BF16)],
        compiler_params=_params("parallel", "arbitrary"),
    )(x, meta, g)


def _matmul_kernel(x_ref, w_ref, cs_ref, o_ref):
    acc = jnp.dot(x_ref[...], w_ref[...], preferred_element_type=F32)
    o_ref[...] = (acc * cs_ref[...]).astype(o_ref.dtype)


def _in_proj(u, w, col_scale):
    R, D = u.shape
    N = w.shape[1]
    tm = _pick(R, (1408, 1024, 768, 512, 384, 256, 128))
    tn = _pick(N, (1024, 512, 256, 128))
    return pl.pallas_call(
        _matmul_kernel,
        grid=(R // tm, N // tn),
        in_specs=[pl.BlockSpec((tm, D), lambda m, n: (m, 0)),
                  pl.BlockSpec((D, tn), lambda m, n: (0, n)),
                  pl.BlockSpec((1, tn), lambda m, n: (0, n))],
        out_specs=pl.BlockSpec((tm, tn), lambda m, n: (m, n)),
        out_shape=jax.ShapeDtypeStruct((R, N), BF16),
        compiler_params=_params("parallel", "arbitrary"),
    )(u, w, col_scale)


ATTN_TQ = 128
ATTN_TK = 256
ATTN_HEAD_GROUP = 4
LOG2E = math.log2(math.e)


def _attn_kernel(slopes_ref, lq_ref, subln_ref, q_ref, k_ref, v_ref, o_ref,
                 q2_ref, acc_ref, l_ref, m_ref, *, n_heads, lam_init):
    tq, tk = ATTN_TQ, ATTN_TK
    hd = A_V_DIM
    qi = pl.program_id(1)

    lane = lax.broadcasted_iota(jnp.int32, (tq, hd), 1)
    for g in range(n_heads):
        q = q_ref[0, :, g * hd:(g + 1) * hd]
        zero = jnp.zeros_like(q)
        q2_ref[g, :tq, :] = jnp.where(lane < A_QK_DIM, q, zero)
        q2_ref[g, tq:, :] = jnp.where(lane >= A_QK_DIM, q, zero)

    m_ref[...] = jnp.full(m_ref.shape, NEG_INF, F32)
    l_ref[...] = jnp.zeros(l_ref.shape, F32)
    acc_ref[...] = jnp.zeros(acc_ref.shape, F32)

    def tile(k0, first_key):
        col = lax.broadcasted_iota(jnp.int32, (1, tk), 1) + k0
        colf = col.astype(F32)
        if first_key is not None:
            row = lax.broadcasted_iota(jnp.int32, (2 * tq, tk), 0)
            qpos = jnp.where(row >= tq, row - tq, row) + qi * tq
            kpos = lax.broadcasted_iota(jnp.int32, (2 * tq, tk), 1) + k0
            visible = (kpos >= first_key) & (kpos <= qpos)
        for g0 in range(0, n_heads, ATTN_HEAD_GROUP):
            head_group(range(g0, min(g0 + ATTN_HEAD_GROUP, n_heads)), k0, first_key, col, colf,
                       visible if first_key is not None else None)

    def head_group(heads, k0, first_key, col, colf, visible):
        cols = {g: slice(g * hd, (g + 1) * hd) for g in heads}
        scores = {g: lax.dot_general(q2_ref[g], k_ref[0, pl.ds(k0, tk), cols[g]],
                                     (((1,), (1,)), ((), ())), preferred_element_type=F32)
                  for g in heads}
        probs, alphas = {}, {}
        for g in heads:
            bias = jnp.where(col >= PAD_FRONT, colf * (slopes_ref[g] * LOG2E), NEG_INF)
            s = scores[g] + bias
            if first_key is not None:
                s = jnp.where(visible, s, NEG_INF)
            m_old = m_ref[g]
            m_new = jnp.maximum(m_old, jnp.max(s, axis=-1, keepdims=True))
            alpha = jnp.exp2(m_old - m_new)
            p_parts = [jnp.exp2(s[:, c:c + hd] - m_new) for c in range(0, tk, hd)]
            l_ref[g] = alpha * l_ref[g] + sum(p_parts[1:], p_parts[0])
            m_ref[g] = m_new
            probs[g] = jnp.concatenate(p_parts, axis=1).astype(BF16)
            alphas[g] = alpha
        for g in heads:
            acc_ref[g] = alphas[g] * acc_ref[g] + jnp.dot(
                probs[g], v_ref[0, pl.ds(k0, tk), cols[g]], preferred_element_type=F32)

    n_full = (qi * tq) // tk

    def body(j, carry):
        tile(pl.multiple_of(j * tk, tk), None)
        return carry

    lax.fori_loop(0, n_full, body, 0)
    tail0 = pl.multiple_of(jnp.maximum(qi * tq + tq - tk, 0), tq)
    tile(tail0, n_full * tk)

    lq = lq_ref[...]
    lam = (jnp.exp(jnp.sum(lq[0:1] * lq[1:2], keepdims=True))
           - jnp.exp(jnp.sum(lq[2:3] * lq[3:4], keepdims=True)) + lam_init)
    subln = subln_ref[...]
    for g in range(n_heads):
        out = acc_ref[g] / jnp.sum(l_ref[g], axis=-1, keepdims=True)
        a = out[:tq] - lam * out[tq:]
        y = _rms(a, subln) * (1.0 - lam_init)
        o_ref[0, :, g * hd:(g + 1) * hd] = y.astype(o_ref.dtype)


def _attention(proj, slopes, lq, subln, lam_init, n_heads):
    B, L, _ = proj.shape
    tq = ATTN_TQ
    hd = A_V_DIM
    d_a = n_heads * hd
    kern = functools.partial(_attn_kernel, n_heads=n_heads, lam_init=lam_init)
    return pl.pallas_call(
        kern,
        grid=(B, L // tq),
        in_specs=[
            pl.BlockSpec(memory_space=pltpu.SMEM),
            pl.BlockSpec((4, A_QK_DIM), lambda b, i: (0, 0)),
            pl.BlockSpec((1, hd), lambda b, i: (0, 0)),
            pl.BlockSpec((1, tq, d_a), lambda b, i: (b, i, 0)),
            pl.BlockSpec((1, L, d_a), lambda b, i: (b, 0, 1)),
            pl.BlockSpec((1, L, d_a), lambda b, i: (b, 0, 2)),
        ],
        out_specs=pl.BlockSpec((1, tq, d_a), lambda b, i: (b, i, 0)),
        out_shape=jax.ShapeDtypeStruct((B, L, d_a), BF16),
        scratch_shapes=[pltpu.VMEM((n_heads, 2 * tq, hd), BF16),
                        pltpu.VMEM((n_heads, 2 * tq, hd), F32),
                        pltpu.VMEM((n_heads, 2 * tq, hd), F32),
                        pltpu.VMEM((n_heads, 2 * tq, hd), F32)],
        compiler_params=_params("parallel", "arbitrary"),
    )(slopes, lq, subln, proj, proj, proj)


def _ret_kernel(logg_ref, rn_ref, q_ref, k_ref, v_ref, g_ref, o_ref, s_ref,
                *, n_heads, n_chunks):
    @pl.when(pl.program_id(1) == 0)
    def _():
        s_ref[...] = jnp.zeros(s_ref.shape, F32)

    C = CHUNK
    ri = lax.broadcasted_iota(jnp.int32, (C, C), 0)
    ci = lax.broadcasted_iota(jnp.int32, (C, C), 1)
    diff = (ri - ci).astype(F32)
    idx = lax.broadcasted_iota(jnp.int32, (C, 1), 0).astype(F32)
    scale = R_QK_DIM ** -0.5
    rn = rn_ref[...]

    for h in range(n_heads):
        lg = logg_ref[h]
        decay_in = jnp.where(diff >= 0, jnp.exp(lg * jnp.maximum(diff, 0.0)), 0.0) * scale
        k_decay = jnp.exp(lg * (C - 1.0 - idx)) * scale
        q_decay = jnp.exp(lg * (idx + 1.0))
        g_chunk = jnp.exp(jnp.full((1, 1), lg * C, F32))
        qk = slice(h * R_QK_DIM, (h + 1) * R_QK_DIM)
        vv = slice(h * R_V_DIM, (h + 1) * R_V_DIM)
        for c in range(n_chunks):
            rows = slice(c * C, (c + 1) * C)
            q = q_ref[0, rows, qk]
            k = k_ref[0, rows, qk]
            v = v_ref[0, rows, vv]
            s = lax.dot_general(q, k, (((1,), (1,)), ((), ())),
                                preferred_element_type=F32) * decay_in
            intra = jnp.dot(s.astype(BF16), v, preferred_element_type=F32)
            s_prev = s_ref[h]
            q_dec = (q.astype(F32) * q_decay).astype(BF16)
            cross = jnp.dot(q_dec, s_prev.astype(BF16), preferred_element_type=F32)
            k_dec = (k.astype(F32) * k_decay).astype(BF16)
            kv = lax.dot_general(k_dec, v, (((0,), (0,)), ((), ())),
                                 preferred_element_type=F32)
            s_ref[h] = g_chunk * s_prev + kv
            y = _rms(intra + cross, rn)
            gate = g_ref[0, rows, vv].astype(F32)
            o_ref[0, rows, vv] = (y * (gate * jax.nn.sigmoid(gate))).astype(o_ref.dtype)


def _retention(proj, log_g, rn, n_heads, col0):
    B, L, _ = proj.shape
    dqk = n_heads * R_QK_DIM
    dv = n_heads * R_V_DIM
    T = _pick(L, (384, 128))
    kern = functools.partial(_ret_kernel, n_heads=n_heads, n_chunks=T // CHUNK)
    q_blk = col0 // dqk
    v_blk = (col0 + 2 * dqk) // dv
    return pl.pallas_call(
        kern,
        grid=(B, L // T),
        in_specs=[
            pl.BlockSpec(memory_space=pltpu.SMEM),
            pl.BlockSpec((1, R_V_DIM), lambda b, n: (0, 0)),
            pl.BlockSpec((1, T, dqk), lambda b, n: (b, n, q_blk)),
            pl.BlockSpec((1, T, dqk), lambda b, n: (b, n, q_blk + 1)),
            pl.BlockSpec((1, T, dv), lambda b, n: (b, n, v_blk)),
            pl.BlockSpec((1, T, dv), lambda b, n: (b, n, v_blk + 1)),
        ],
        out_specs=pl.BlockSpec((1, T, dv), lambda b, n: (b, n, 0)),
        out_shape=jax.ShapeDtypeStruct((B, L, dv), BF16),
        scratch_shapes=[pltpu.VMEM((n_heads, R_QK_DIM, R_V_DIM), F32)],
        compiler_params=_params("parallel", "arbitrary"),
    )(log_g, rn, proj, proj, proj, proj)


def _out_proj_kernel(ya_ref, yr_ref, w_ref, h_ref, g_ref, ho_ref, u_ref,
                     *, tm, tiles_per_seq):
    da = ya_ref.shape[1]
    acc = jnp.dot(ya_ref[...], w_ref[:da, :], preferred_element_type=F32)
    acc = acc + jnp.dot(yr_ref[...], w_ref[da:, :], preferred_element_type=F32)
    hn = h_ref[...] + acc
    ho_ref[...] = hn
    seq_row0 = (pl.program_id(0) % tiles_per_seq) * tm
    u_ref[...] = _masked_norm(hn, g_ref[...], seq_row0).astype(u_ref.dtype)


def _out_proj(ya, yr, w, h, g, L):
    R, D = h.shape
    da, dr = ya.shape[1], yr.shape[1]
    tm = _pick(L, (528, 384, 128))
    kern = functools.partial(_out_proj_kernel, tm=tm, tiles_per_seq=L // tm)
    return pl.pallas_call(
        kern,
        grid=(R // tm,),
        in_specs=[
            pl.BlockSpec((tm, da), lambda m: (m, 0)),
            pl.BlockSpec((tm, dr), lambda m: (m, 0)),
            pl.BlockSpec((da + dr, D), lambda m: (0, 0)),
            pl.BlockSpec((tm, D), lambda m: (m, 0)),
            pl.BlockSpec((1, D), lambda m: (0, 0)),
        ],
        out_specs=[pl.BlockSpec((tm, D), lambda m: (m, 0)),
                   pl.BlockSpec((tm, D), lambda m: (m, 0))],
        out_shape=[jax.ShapeDtypeStruct((R, D), F32),
                   jax.ShapeDtypeStruct((R, D), BF16)],
        input_output_aliases={3: 0},
        compiler_params=_params("parallel"),
    )(ya, yr, w, h, g)


def _ffn_kernel(u_ref, wg_ref, wu_ref, cw_ref, wd_ref, h_ref, g_ref, *rest,
                tm, tiles_per_seq, final):
    if final:
        ho_ref, gbuf_ref, carry_ref = rest
    else:
        ho_ref, u_out_ref, gbuf_ref, carry_ref = rest
    m = pl.program_id(0)
    f = pl.program_id(1)
    seq_tile = m % tiles_per_seq
    tf = wg_ref.shape[1]

    @pl.when(seq_tile == 0)
    def _():
        gbuf_ref[:CONV_HALO, :] = jnp.zeros((CONV_HALO, tf), F32)

    @pl.when(seq_tile != 0)
    def _():
        gbuf_ref[:CONV_HALO, :] = carry_ref[f]

    @pl.when(f == 0)
    def _():
        ho_ref[...] = h_ref[...]

    u = u_ref[...]
    subs = [slice(c, c + FFN_SUB) for c in range(0, tf, FFN_SUB)]
    gates = [jnp.dot(u, wg_ref[:, cs], preferred_element_type=F32) for cs in subs]
    ups = [jnp.dot(u, wu_ref[:, cs], preferred_element_type=F32) for cs in subs]
    for cs, gate, up in zip(subs, gates, ups):
        gbuf_ref[CONV_HALO:, cs] = gate
        carry_ref[f, :, cs] = gate[tm - CONV_HALO:, :]
        cw = cw_ref[:, cs]
        gc = gate * cw[CONV_W - 1:CONV_W, :]
        for i in range(CONV_W - 1):
            back = CONV_W - 1 - i
            gc = gc + gbuf_ref[pl.ds(CONV_HALO - back, tm), cs] * cw[i:i + 1, :]
        act = (gc * jax.nn.sigmoid(gc)) * up
        ho_ref[...] += jnp.dot(act.astype(BF16), wd_ref[cs, :], preferred_element_type=F32)

    @pl.when(f == pl.num_programs(1) - 1)
    def _():
        hn = ho_ref[...]
        g = g_ref[...]
        if final:
            ho_ref[...] = _rms(hn, g)
        else:
            u_out_ref[...] = _masked_norm(hn, g, seq_tile * tm).astype(u_out_ref.dtype)


def _ffn(u, wg, wu, cw, wd, h, g, L, final):
    R, D = h.shape
    F = wg.shape[1]
    tm = _pick(L, (704, 384, 128))
    tf = _pick(F, (512, 256, 128))
    kern = functools.partial(_ffn_kernel, tm=tm, tiles_per_seq=L // tm, final=final)
    row_spec = pl.BlockSpec((tm, D), lambda m, f: (m, 0))
    out_specs = [row_spec]
    out_shape = [jax.ShapeDtypeStruct((R, D), F32)]
    if not final:
        out_specs.append(row_spec)
        out_shape.append(jax.ShapeDtypeStruct((R, D), BF16))
    return pl.pallas_call(
        kern,
        grid=(R // tm, F // tf),
        in_specs=[
            row_spec,
            pl.BlockSpec((D, tf), lambda m, f: (0, f)),
            pl.BlockSpec((D, tf), lambda m, f: (0, f)),
            pl.BlockSpec((CONV_W, tf), lambda m, f: (0, f)),
            pl.BlockSpec((tf, D), lambda m, f: (f, 0)),
            row_spec,
            pl.BlockSpec((1, D), lambda m, f: (0, 0)),
        ],
        out_specs=out_specs,
        out_shape=out_shape,
        scratch_shapes=[pltpu.VMEM((tm + CONV_HALO, tf), F32),
                        pltpu.VMEM((F // tf, CONV_HALO, tf), F32)],
        input_output_aliases={5: 0},
        compiler_params=_params("arbitrary", "arbitrary"),
    )(u, wg, wu, cw, wd, h, g)


def kernel(x, meta_tokens, attn_norm, w_in, lambda_qk, attn_subln, ret_norm, w_out,
           ffn_norm, w_gate, w_up, conv_w, w_down, final_norm):
    B, S, D = x.shape
    L = CHUNK + S
    R = B * L
    depth = w_in.shape[0]
    a_heads = D // 256
    r_heads = D // 512
    d_a = a_heads * A_V_DIM
    ret_col0 = 2 * a_heads * 2 * A_QK_DIM + d_a

    slopes = 2.0 ** (-8.0 * jnp.arange(1, a_heads + 1, dtype=F32) / a_heads)
    log_g = jnp.log1p(-(2.0 ** (-5.0 - jnp.arange(r_heads, dtype=F32))))

    n_aq = a_heads * 2 * A_QK_DIM
    col_scale = jnp.where(jnp.arange(w_in.shape[2]) < n_aq,
                          LOG2E * A_QK_DIM ** -0.5, 1.0).astype(F32)[None, :]

    w_in_b = w_in.astype(BF16)
    w_out_b = w_out.astype(BF16)
    w_gate_b = w_gate.astype(BF16)
    w_up_b = w_up.astype(BF16)
    w_down_b = w_down.astype(BF16)

    h, u = _prep(x, meta_tokens, attn_norm[0:1])
    h = h.reshape(R, D)
    u = u.reshape(R, D)
    for l in range(depth):
        lam_init = 0.8 - 0.6 * math.exp(-0.3 * l)
        proj = _in_proj(u, w_in_b[l], col_scale).reshape(B, L, -1)
        ya = _attention(proj, slopes, lambda_qk[l], attn_subln[l:l + 1], lam_init, a_heads)
        yr = _retention(proj, log_g, ret_norm[l:l + 1], r_heads, ret_col0)
        h, u2 = _out_proj(ya.reshape(R, -1), yr.reshape(R, -1), w_out_b[l], h,
                          ffn_norm[l:l + 1], L)
        final = l == depth - 1
        g_next = final_norm[None, :] if final else attn_norm[l + 1:l + 2]
        res = _ffn(u2, w_gate_b[l], w_up_b[l], conv_w[l], w_down_b[l], h, g_next, L, final)
        if final:
            h = res[0]
        else:
            h, u = res
    return h.reshape(B, L, D)[:, CHUNK:]
```

```python
import functools
import math

import jax
import jax.numpy as jnp
from jax import lax
from jax.experimental import pallas as pl
from jax.experimental.pallas import tpu as pltpu

N_META = 16
CHUNK = 128
PAD_FRONT = CHUNK - N_META
A_QK_DIM = 64
A_V_DIM = 128
R_QK_DIM = 128
R_V_DIM = 256
CONV_W = 3
EPS = 1e-6
NEG_INF = -1e30
CONV_HALO = 8
FFN_SUB = 256

V7X_VMEM_BYTES = 64 * 1024 * 1024
VMEM_LIMIT_BYTES = V7X_VMEM_BYTES - 8 * 1024 * 1024

F32 = jnp.float32
BF16 = jnp.bfloat16


def _pick(n, candidates):
    for c in candidates:
        if n % c == 0:
            return c
    raise ValueError(f"no tile in {candidates} divides {n}")


def _params(*semantics):
    return pltpu.CompilerParams(dimension_semantics=semantics,
                                vmem_limit_bytes=VMEM_LIMIT_BYTES)


def _rms(xf, g):
    return xf * lax.rsqrt(jnp.mean(xf * xf, axis=-1, keepdims=True) + EPS) * g


def _masked_norm(hn, g, seq_row0):
    row = lax.broadcasted_iota(jnp.int32, hn.shape, 0) + seq_row0
    return jnp.where(row >= PAD_FRONT, _rms(hn, g), 0.0)


def _prep_kernel(x_ref, meta_ref, g_ref, h_ref, u_ref):
    i = pl.program_id(1)
    g = g_ref[...]

    @pl.when(i == 0)
    def _():
        meta = meta_ref[...]
        h_ref[0, :PAD_FRONT, :] = jnp.zeros((PAD_FRONT, meta.shape[1]), F32)
        h_ref[0, PAD_FRONT:, :] = meta
        u_ref[0, :PAD_FRONT, :] = jnp.zeros((PAD_FRONT, meta.shape[1]), BF16)
        u_ref[0, PAD_FRONT:, :] = _rms(meta, g).astype(BF16)

    @pl.when(i > 0)
    def _():
        x = x_ref[0]
        h_ref[0] = x
        u_ref[0] = _rms(x, g).astype(BF16)


def _prep(x, meta, g):
    B, S, D = x.shape
    L = CHUNK + S
    return pl.pallas_call(
        _prep_kernel,
        grid=(B, L // CHUNK),
        in_specs=[
            pl.BlockSpec((1, CHUNK, D), lambda b, i: (b, jnp.maximum(i - 1, 0), 0)),
            pl.BlockSpec((N_META, D), lambda b, i: (0, 0)),
            pl.BlockSpec((1, D), lambda b, i: (0, 0)),
        ],
        out_specs=[
            pl.BlockSpec((1, CHUNK, D), lambda b, i: (b, i, 0)),
            pl.BlockSpec((1, CHUNK, D), lambda b, i: (b, i, 0)),
        ],
        out_shape=[jax.ShapeDtypeStruct((B, L, D), F32),
                   jax.ShapeDtypeStruct((B, L, D), BF16)],
        compiler_params=_params("parallel", "arbitrary"),
    )(x, meta, g)


def _matmul_kernel(x_ref, w_ref, cs_ref, o_ref):
    acc = jnp.dot(x_ref[...], w_ref[...], preferred_element_type=F32)
    o_ref[...] = (acc * cs_ref[...]).astype(o_ref.dtype)


def _in_proj(u, w, col_scale):
    R, D = u.shape
    N = w.shape[1]
    tm = _pick(R, (1408, 1024, 768, 512, 384, 256, 128))
    tn = _pick(N, (1024, 512, 256, 128))
    return pl.pallas_call(
        _matmul_kernel,
        grid=(R // tm, N // tn),
        in_specs=[pl.BlockSpec((tm, D), lambda m, n: (m, 0)),
                  pl.BlockSpec((D, tn), lambda m, n: (0, n)),
                  pl.BlockSpec((1, tn), lambda m, n: (0, n))],
        out_specs=pl.BlockSpec((tm, tn), lambda m, n: (m, n)),
        out_shape=jax.ShapeDtypeStruct((R, N), BF16),
        compiler_params=_params("parallel", "arbitrary"),
    )(u, w, col_scale)


ATTN_TQ = 128
ATTN_TK = 512
ATTN_HEAD_GROUP = 8
LOG2E = math.log2(math.e)


def _attn_kernel(slopes_ref, lq_ref, subln_ref, q_ref, k_ref, v_ref, o_ref,
                 vt_ref, bias_ref, q2t_ref, acc_ref, l_ref, m_ref, *, n_heads, lam_init):
    tq, tk = ATTN_TQ, ATTN_TK
    hd = A_V_DIM
    qi = pl.program_id(1)
    L = k_ref.shape[1]
    heads = range(n_heads)
    cols = [slice(g * hd, (g + 1) * hd) for g in heads]
    blocks_per_tile = tk // hd

    @pl.when(qi == 0)
    def _():
        def vt_block(j, carry):
            r0 = pl.multiple_of(j * hd, hd)
            for g in heads:
                vt_ref[g, j] = v_ref[0, pl.ds(r0, hd), cols[g]].astype(F32).T.astype(BF16)
            return carry
        lax.fori_loop(0, L // hd, vt_block, 0)
        rel = lax.broadcasted_iota(jnp.int32, (tk, 2 * tq), 0)
        relf = rel.astype(F32)
        for g in heads:
            b = relf * (slopes_ref[g] * LOG2E)
            bias_ref[0, g] = b
            bias_ref[1, g] = jnp.where(rel >= PAD_FRONT, b, NEG_INF)

    dim = lax.broadcasted_iota(jnp.int32, (hd, tq), 0)
    for g in heads:
        qt = q_ref[0, :, cols[g]].astype(F32).T.astype(BF16)
        zero = jnp.zeros_like(qt)
        q2t_ref[g] = jnp.concatenate([jnp.where(dim < A_QK_DIM, qt, zero),
                                      jnp.where(dim >= A_QK_DIM, qt, zero)], axis=1)

    m_ref[...] = jnp.full(m_ref.shape, NEG_INF, F32)
    l_ref[...] = jnp.zeros(l_ref.shape, F32)
    acc_ref[...] = jnp.zeros(acc_ref.shape, F32)

    def tile(k0, first_key):
        variant = (k0 == 0).astype(jnp.int32)
        blk0 = k0 // hd
        k0f = k0.astype(F32)
        visible = None
        if first_key is not None:
            kpos = lax.broadcasted_iota(jnp.int32, (tk, 2 * tq), 0) + k0
            lane = lax.broadcasted_iota(jnp.int32, (tk, 2 * tq), 1)
            qpos = jnp.where(lane >= tq, lane - tq, lane) + qi * tq
            visible = (kpos >= first_key) & (kpos <= qpos)
        for g0 in range(0, n_heads, ATTN_HEAD_GROUP):
            head_group(range(g0, min(g0 + ATTN_HEAD_GROUP, n_heads)), k0, k0f, blk0, variant, visible)

    def head_group(group, k0, k0f, blk0, variant, visible):
        scores = {g: jnp.dot(k_ref[0, pl.ds(k0, tk), cols[g]], q2t_ref[g],
                             preferred_element_type=F32) for g in group}
        probs, alphas = {}, {}
        for g in group:
            t = scores[g] + bias_ref[variant, g]
            if visible is not None:
                t = jnp.where(visible, t, NEG_INF)
            shift = k0f * (slopes_ref[g] * LOG2E)
            m_old = m_ref[g] - shift
            m_new = jnp.maximum(m_old, jnp.max(t, axis=0, keepdims=True))
            alpha = jnp.exp2(m_old - m_new)
            p = jnp.exp2(t - m_new)
            l_ref[g] = alpha * l_ref[g] + jnp.sum(p, axis=0, keepdims=True)
            m_ref[g] = m_new + shift
            probs[g] = p.astype(BF16)
            alphas[g] = alpha
        for g in group:
            vt = jnp.concatenate([vt_ref[g, blk0 + i] for i in range(blocks_per_tile)], axis=1)
            acc_ref[g] = alphas[g] * acc_ref[g] + jnp.dot(vt, probs[g],
                                                          preferred_element_type=F32)

    n_full = (qi * tq) // tk

    def body(j, carry):
        tile(pl.multiple_of(j * tk, tk), None)
        return carry

    lax.fori_loop(0, n_full, body, 0)
    tail0 = pl.multiple_of(jnp.maximum(qi * tq + tq - tk, 0), tq)
    tile(tail0, n_full * tk)

    lq = lq_ref[...]
    lam = (jnp.exp(jnp.sum(lq[0:1] * lq[1:2], keepdims=True))
           - jnp.exp(jnp.sum(lq[2:3] * lq[3:4], keepdims=True)) + lam_init)
    subln = subln_ref[...]
    for g in heads:
        out_t = acc_ref[g] / l_ref[g]
        a = (out_t[:, :tq] - lam * out_t[:, tq:]).T
        y = _rms(a, subln) * (1.0 - lam_init)
        o_ref[0, :, cols[g]] = y.astype(o_ref.dtype)


def _attention(proj, slopes, lq, subln, lam_init, n_heads):
    B, L, _ = proj.shape
    tq, tk = ATTN_TQ, ATTN_TK
    hd = A_V_DIM
    d_a = n_heads * hd
    kern = functools.partial(_attn_kernel, n_heads=n_heads, lam_init=lam_init)
    return pl.pallas_call(
        kern,
        grid=(B, L // tq),
        in_specs=[
            pl.BlockSpec(memory_space=pltpu.SMEM),
            pl.BlockSpec((4, A_QK_DIM), lambda b, i: (0, 0)),
            pl.BlockSpec((1, hd), lambda b, i: (0, 0)),
            pl.BlockSpec((1, tq, d_a), lambda b, i: (b, i, 0)),
            pl.BlockSpec((1, L, d_a), lambda b, i: (b, 0, 1)),
            pl.BlockSpec((1, L, d_a), lambda b, i: (b, 0, 2), pipeline_mode=pl.Buffered(1)),
        ],
        out_specs=pl.BlockSpec((1, tq, d_a), lambda b, i: (b, i, 0)),
        out_shape=jax.ShapeDtypeStruct((B, L, d_a), BF16),
        scratch_shapes=[pltpu.VMEM((n_heads, L // hd, hd, hd), BF16),
                        pltpu.VMEM((2, n_heads, tk, 2 * tq), F32),
                        pltpu.VMEM((n_heads, hd, 2 * tq), BF16),
                        pltpu.VMEM((n_heads, hd, 2 * tq), F32),
                        pltpu.VMEM((n_heads, 1, 2 * tq), F32),
                        pltpu.VMEM((n_heads, 1, 2 * tq), F32)],
        compiler_params=_params("parallel", "arbitrary"),
    )(slopes, lq, subln, proj, proj, proj)


def _ret_kernel(logg_ref, rn_ref, q_ref, k_ref, v_ref, g_ref, o_ref, s_ref,
                *, n_heads, n_chunks):
    @pl.when(pl.program_id(1) == 0)
    def _():
        s_ref[...] = jnp.zeros(s_ref.shape, F32)

    C = CHUNK
    ri = lax.broadcasted_iota(jnp.int32, (C, C), 0)
    ci = lax.broadcasted_iota(jnp.int32, (C, C), 1)
    diff = (ri - ci).astype(F32)
    idx = lax.broadcasted_iota(jnp.int32, (C, 1), 0).astype(F32)
    scale = R_QK_DIM ** -0.5
    rn = rn_ref[...]

    for h in range(n_heads):
        lg = logg_ref[h]
        decay_in = jnp.where(diff >= 0, jnp.exp(lg * jnp.maximum(diff, 0.0)), 0.0) * scale
        k_decay = jnp.exp(lg * (C - 1.0 - idx)) * scale
        q_decay = jnp.exp(lg * (idx + 1.0))
        g_chunk = jnp.exp(jnp.full((1, 1), lg * C, F32))
        qk = slice(h * R_QK_DIM, (h + 1) * R_QK_DIM)
        vv = slice(h * R_V_DIM, (h + 1) * R_V_DIM)
        for c in range(n_chunks):
            rows = slice(c * C, (c + 1) * C)
            q = q_ref[0, rows, qk]
            k = k_ref[0, rows, qk]
            v = v_ref[0, rows, vv]
            s = lax.dot_general(q, k, (((1,), (1,)), ((), ())),
                                preferred_element_type=F32) * decay_in
            intra = jnp.dot(s.astype(BF16), v, preferred_element_type=F32)
            s_prev = s_ref[h]
            q_dec = (q.astype(F32) * q_decay).astype(BF16)
            cross = jnp.dot(q_dec, s_prev.astype(BF16), preferred_element_type=F32)
            k_dec = (k.astype(F32) * k_decay).astype(BF16)
            kv = lax.dot_general(k_dec, v, (((0,), (0,)), ((), ())),
                                 preferred_element_type=F32)
            s_ref[h] = g_chunk * s_prev + kv
            y = _rms(intra + cross, rn)
            gate = g_ref[0, rows, vv].astype(F32)
            o_ref[0, rows, vv] = (y * (gate * jax.nn.sigmoid(gate))).astype(o_ref.dtype)


def _retention(proj, log_g, rn, n_heads, col0):
    B, L, _ = proj.shape
    dqk = n_heads * R_QK_DIM
    dv = n_heads * R_V_DIM
    T = _pick(L, (384, 128))
    kern = functools.partial(_ret_kernel, n_heads=n_heads, n_chunks=T // CHUNK)
    q_blk = col0 // dqk
    v_blk = (col0 + 2 * dqk) // dv
    return pl.pallas_call(
        kern,
        grid=(B, L // T),
        in_specs=[
            pl.BlockSpec(memory_space=pltpu.SMEM),
            pl.BlockSpec((1, R_V_DIM), lambda b, n: (0, 0)),
            pl.BlockSpec((1, T, dqk), lambda b, n: (b, n, q_blk)),
            pl.BlockSpec((1, T, dqk), lambda b, n: (b, n, q_blk + 1)),
            pl.BlockSpec((1, T, dv), lambda b, n: (b, n, v_blk)),
            pl.BlockSpec((1, T, dv), lambda b, n: (b, n, v_blk + 1)),
        ],
        out_specs=pl.BlockSpec((1, T, dv), lambda b, n: (b, n, 0)),
        out_shape=jax.ShapeDtypeStruct((B, L, dv), BF16),
        scratch_shapes=[pltpu.VMEM((n_heads, R_QK_DIM, R_V_DIM), F32)],
        compiler_params=_params("parallel", "arbitrary"),
    )(log_g, rn, proj, proj, proj, proj)


def _out_proj_kernel(ya_ref, yr_ref, w_ref, h_ref, g_ref, ho_ref, u_ref,
                     *, tm, tiles_per_seq):
    da = ya_ref.shape[1]
    acc = jnp.dot(ya_ref[...], w_ref[:da, :], preferred_element_type=F32)
    acc = acc + jnp.dot(yr_ref[...], w_ref[da:, :], preferred_element_type=F32)
    hn = h_ref[...] + acc
    ho_ref[...] = hn
    seq_row0 = (pl.program_id(0) % tiles_per_seq) * tm
    u_ref[...] = _masked_norm(hn, g_ref[...], seq_row0).astype(u_ref.dtype)


def _out_proj(ya, yr, w, h, g, L):
    R, D = h.shape
    da, dr = ya.shape[1], yr.shape[1]
    tm = _pick(L, (528, 384, 128))
    kern = functools.partial(_out_proj_kernel, tm=tm, tiles_per_seq=L // tm)
    return pl.pallas_call(
        kern,
        grid=(R // tm,),
        in_specs=[
            pl.BlockSpec((tm, da), lambda m: (m, 0)),
            pl.BlockSpec((tm, dr), lambda m: (m, 0)),
            pl.BlockSpec((da + dr, D), lambda m: (0, 0)),
            pl.BlockSpec((tm, D), lambda m: (m, 0)),
            pl.BlockSpec((1, D), lambda m: (0, 0)),
        ],
        out_specs=[pl.BlockSpec((tm, D), lambda m: (m, 0)),
                   pl.BlockSpec((tm, D), lambda m: (m, 0))],
        out_shape=[jax.ShapeDtypeStruct((R, D), F32),
                   jax.ShapeDtypeStruct((R, D), BF16)],
        input_output_aliases={3: 0},
        compiler_params=_params("parallel"),
    )(ya, yr, w, h, g)


def _ffn_kernel(u_ref, wg_ref, wu_ref, cw_ref, wd_ref, h_ref, g_ref, *rest,
                tm, tiles_per_seq, final):
    if final:
        ho_ref, gbuf_ref, carry_ref = rest
    else:
        ho_ref, u_out_ref, gbuf_ref, carry_ref = rest
    m = pl.program_id(0)
    f = pl.program_id(1)
    seq_tile = m % tiles_per_seq
    tf = wg_ref.shape[1]

    @pl.when(seq_tile == 0)
    def _():
        gbuf_ref[:CONV_HALO, :] = jnp.zeros((CONV_HALO, tf), F32)

    @pl.when(seq_tile != 0)
    def _():
        gbuf_ref[:CONV_HALO, :] = carry_ref[f]

    @pl.when(f == 0)
    def _():
        ho_ref[...] = h_ref[...]

    u = u_ref[...]
    subs = [slice(c, c + FFN_SUB) for c in range(0, tf, FFN_SUB)]
    gates = [jnp.dot(u, wg_ref[:, cs], preferred_element_type=F32) for cs in subs]
    ups = [jnp.dot(u, wu_ref[:, cs], preferred_element_type=F32) for cs in subs]
    for cs, gate, up in zip(subs, gates, ups):
        gbuf_ref[CONV_HALO:, cs] = gate
        carry_ref[f, :, cs] = gate[tm - CONV_HALO:, :]
        cw = cw_ref[:, cs]
        gc = gate * cw[CONV_W - 1:CONV_W, :]
        for i in range(CONV_W - 1):
            back = CONV_W - 1 - i
            gc = gc + gbuf_ref[pl.ds(CONV_HALO - back, tm), cs] * cw[i:i + 1, :]
        act = (gc * jax.nn.sigmoid(gc)) * up
        ho_ref[...] += jnp.dot(act.astype(BF16), wd_ref[cs, :], preferred_element_type=F32)

    @pl.when(f == pl.num_programs(1) - 1)
    def _():
        hn = ho_ref[...]
        g = g_ref[...]
        if final:
            ho_ref[...] = _rms(hn, g)
        else:
            u_out_ref[...] = _masked_norm(hn, g, seq_tile * tm).astype(u_out_ref.dtype)


def _ffn(u, wg, wu, cw, wd, h, g, L, final):
    R, D = h.shape
    F = wg.shape[1]
    tm = _pick(L, (704, 384, 128))
    tf = _pick(F, (512, 256, 128))
    kern = functools.partial(_ffn_kernel, tm=tm, tiles_per_seq=L // tm, final=final)
    row_spec = pl.BlockSpec((tm, D), lambda m, f: (m, 0))
    out_specs = [row_spec]
    out_shape = [jax.ShapeDtypeStruct((R, D), F32)]
    if not final:
        out_specs.append(row_spec)
        out_shape.append(jax.ShapeDtypeStruct((R, D), BF16))
    return pl.pallas_call(
        kern,
        grid=(R // tm, F // tf),
        in_specs=[
            row_spec,
            pl.BlockSpec((D, tf), lambda m, f: (0, f)),
            pl.BlockSpec((D, tf), lambda m, f: (0, f)),
            pl.BlockSpec((CONV_W, tf), lambda m, f: (0, f)),
            pl.BlockSpec((tf, D), lambda m, f: (f, 0)),
            row_spec,
            pl.BlockSpec((1, D), lambda m, f: (0, 0)),
        ],
        out_specs=out_specs,
        out_shape=out_shape,
        scratch_shapes=[pltpu.VMEM((tm + CONV_HALO, tf), F32),
                        pltpu.VMEM((F // tf, CONV_HALO, tf), F32)],
        input_output_aliases={5: 0},
        compiler_params=_params("arbitrary", "arbitrary"),
    )(u, wg, wu, cw, wd, h, g)


def kernel(x, meta_tokens, attn_norm, w_in, lambda_qk, attn_subln, ret_norm, w_out,
           ffn_norm, w_gate, w_up, conv_w, w_down, final_norm):
    B, S, D = x.shape
    L = CHUNK + S
    R = B * L
    depth = w_in.shape[0]
    a_heads = D // 256
    r_heads = D // 512
    d_a = a_heads * A_V_DIM
    ret_col0 = 2 * a_heads * 2 * A_QK_DIM + d_a

    slopes = 2.0 ** (-8.0 * jnp.arange(1, a_heads + 1, dtype=F32) / a_heads)
    log_g = jnp.log1p(-(2.0 ** (-5.0 - jnp.arange(r_heads, dtype=F32))))

    n_aq = a_heads * 2 * A_QK_DIM
    col_scale = jnp.where(jnp.arange(w_in.shape[2]) < n_aq,
                          LOG2E * A_QK_DIM ** -0.5, 1.0).astype(F32)[None, :]

    w_in_b = w_in.astype(BF16)
    w_out_b = w_out.astype(BF16)
    w_gate_b = w_gate.astype(BF16)
    w_up_b = w_up.astype(BF16)
    w_down_b = w_down.astype(BF16)

    h, u = _prep(x, meta_tokens, attn_norm[0:1])
    h = h.reshape(R, D)
    u = u.reshape(R, D)
    for l in range(depth):
        lam_init = 0.8 - 0.6 * math.exp(-0.3 * l)
        proj = _in_proj(u, w_in_b[l], col_scale).reshape(B, L, -1)
        ya = _attention(proj, slopes, lambda_qk[l], attn_subln[l:l + 1], lam_init, a_heads)
        yr = _retention(proj, log_g, ret_norm[l:l + 1], r_heads, ret_col0)
        h, u2 = _out_proj(ya.reshape(R, -1), yr.reshape(R, -1), w_out_b[l], h,
                          ffn_norm[l:l + 1], L)
        final = l == depth - 1
        g_next = final_norm[None, :] if final else attn_norm[l + 1:l + 2]
        res = _ffn(u2, w_gate_b[l], w_up_b[l], conv_w[l], w_down_b[l], h, g_next, L, final)
        if final:
            h = res[0]
        else:
            h, u = res
    return h.reshape(B, L, D)[:, CHUNK:]
```

```python
import functools
import math

import jax
import jax.numpy as jnp
from jax import lax
from jax.experimental import pallas as pl
from jax.experimental.pallas import tpu as pltpu

N_META = 16
CHUNK = 128
PAD_FRONT = CHUNK - N_META
A_QK_DIM = 64
A_V_DIM = 128
R_QK_DIM = 128
R_V_DIM = 256
CONV_W = 3
EPS = 1e-6
NEG_INF = -1e30
CONV_HALO = 8
FFN_SUB = 256

V7X_VMEM_BYTES = 64 * 1024 * 1024
VMEM_LIMIT_BYTES = V7X_VMEM_BYTES - 8 * 1024 * 1024

F32 = jnp.float32
BF16 = jnp.bfloat16


def _pick(n, candidates):
    for c in candidates:
        if n % c == 0:
            return c
    raise ValueError(f"no tile in {candidates} divides {n}")


def _params(*semantics):
    return pltpu.CompilerParams(dimension_semantics=semantics,
                                vmem_limit_bytes=VMEM_LIMIT_BYTES)


def _rms(xf, g):
    return xf * lax.rsqrt(jnp.mean(xf * xf, axis=-1, keepdims=True) + EPS) * g


def _masked_norm(hn, g, seq_row0):
    row = lax.broadcasted_iota(jnp.int32, hn.shape, 0) + seq_row0
    return jnp.where(row >= PAD_FRONT, _rms(hn, g), 0.0)


def _prep_kernel(x_ref, meta_ref, g_ref, h_ref, u_ref):
    i = pl.program_id(1)
    g = g_ref[...]

    @pl.when(i == 0)
    def _():
        meta = meta_ref[...]
        h_ref[0, :PAD_FRONT, :] = jnp.zeros((PAD_FRONT, meta.shape[1]), F32)
        h_ref[0, PAD_FRONT:, :] = meta
        u_ref[0, :PAD_FRONT, :] = jnp.zeros((PAD_FRONT, meta.shape[1]), BF16)
        u_ref[0, PAD_FRONT:, :] = _rms(meta, g).astype(BF16)

    @pl.when(i > 0)
    def _():
        x = x_ref[0]
        h_ref[0] = x
        u_ref[0] = _rms(x, g).astype(BF16)


def _prep(x, meta, g):
    B, S, D = x.shape
    L = CHUNK + S
    return pl.pallas_call(
        _prep_kernel,
        grid=(B, L // CHUNK),
        in_specs=[
            pl.BlockSpec((1, CHUNK, D), lambda b, i: (b, jnp.maximum(i - 1, 0), 0)),
            pl.BlockSpec((N_META, D), lambda b, i: (0, 0)),
            pl.BlockSpec((1, D), lambda b, i: (0, 0)),
        ],
        out_specs=[
            pl.BlockSpec((1, CHUNK, D), lambda b, i: (b, i, 0)),
            pl.BlockSpec((1, CHUNK, D), lambda b, i: (b, i, 0)),
        ],
        out_shape=[jax.ShapeDtypeStruct((B, L, D), F32),
                   jax.ShapeDtypeStruct((B, L, D), BF16)],
        compiler_params=_params("parallel", "arbitrary"),
    )(x, meta, g)


def _matmul_kernel(x_ref, w_ref, cs_ref, o_ref):
    acc = jnp.dot(x_ref[...], w_ref[...], preferred_element_type=F32)
    o_ref[...] = (acc * cs_ref[...]).astype(o_ref.dtype)


def _in_proj(u, w, col_scale):
    R, D = u.shape
    N = w.shape[1]
    tm = _pick(R, (1408, 1024, 768, 512, 384, 256, 128))
    tn = _pick(N, (1024, 512, 256, 128))
    return pl.pallas_call(
        _matmul_kernel,
        grid=(R // tm, N // tn),
        in_specs=[pl.BlockSpec((tm, D), lambda m, n: (m, 0)),
                  pl.BlockSpec((D, tn), lambda m, n: (0, n)),
                  pl.BlockSpec((1, tn), lambda m, n: (0, n))],
        out_specs=pl.BlockSpec((tm, tn), lambda m, n: (m, n)),
        out_shape=jax.ShapeDtypeStruct((R, N), BF16),
        compiler_params=_params("parallel", "arbitrary"),
    )(u, w, col_scale)


ATTN_TQ = 128
ATTN_TK = 512
BIAS_LANES = 4
LOG2E = math.log2(math.e)


def _attn_kernel(slopes_ref, lq_ref, subln_ref, q_ref, k_ref, v_ref, o_ref,
                 vt_ref, kb_ref, q2t_ref, acc_ref, l_ref, m_ref, *, n_heads, lam_init):
    tq, tk = ATTN_TQ, ATTN_TK
    hd = A_V_DIM
    qi = pl.program_id(1)
    L = k_ref.shape[1]
    heads = range(n_heads)
    cols = [slice(g * hd, (g + 1) * hd) for g in heads]
    blocks_per_tile = tk // hd

    @pl.when(qi == 0)
    def _():
        def vt_block(j, carry):
            r0 = pl.multiple_of(j * hd, hd)
            for g in heads:
                vt_ref[g, j] = v_ref[0, pl.ds(r0, hd), cols[g]].astype(F32).T.astype(BF16)
            return carry
        lax.fori_loop(0, L // hd, vt_block, 0)
        kpos = lax.broadcasted_iota(jnp.int32, (L, hd), 0)
        lane = lax.broadcasted_iota(jnp.int32, (L, hd), 1)
        slope = jnp.zeros((L, hd), F32)
        for g in heads:
            slope = jnp.where(lane // BIAS_LANES == g, slopes_ref[g] * LOG2E, slope)
        rest = kpos.astype(F32) * slope
        term = lane % BIAS_LANES
        kb = jnp.where((term == BIAS_LANES - 1) & (lane < n_heads * BIAS_LANES) & (kpos < PAD_FRONT),
                       NEG_INF, 0.0)
        for t in range(BIAS_LANES - 1):
            part = rest.astype(BF16).astype(F32)
            kb = jnp.where(term == t, part, kb)
            rest = rest - part
        kb_ref[...] = kb.astype(BF16)
        row = lax.broadcasted_iota(jnp.int32, (hd, 2 * tq), 0)
        for g in heads:
            q2t_ref[g, hd:, :] = jnp.where(row // BIAS_LANES == g, 1.0, 0.0).astype(BF16)

    dim = lax.broadcasted_iota(jnp.int32, (hd, tq), 0)
    for g in heads:
        qt = q_ref[0, :, cols[g]].astype(F32).T.astype(BF16)
        zero = jnp.zeros_like(qt)
        q2t_ref[g, :hd, :] = jnp.concatenate([jnp.where(dim < A_QK_DIM, qt, zero),
                                              jnp.where(dim >= A_QK_DIM, qt, zero)], axis=1)

    m_ref[...] = jnp.full(m_ref.shape, NEG_INF, F32)
    l_ref[...] = jnp.zeros(l_ref.shape, F32)
    acc_ref[...] = jnp.zeros(acc_ref.shape, F32)

    def tile(k0, first_key):
        blk0 = k0 // hd
        if first_key is not None:
            kpos = lax.broadcasted_iota(jnp.int32, (tk, 2 * tq), 0) + k0
            lane = lax.broadcasted_iota(jnp.int32, (tk, 2 * tq), 1)
            qpos = jnp.where(lane >= tq, lane - tq, lane) + qi * tq
            visible = (kpos >= first_key) & (kpos <= qpos)
        kb = kb_ref[pl.ds(k0, tk), :]
        scores = [jnp.dot(jnp.concatenate([k_ref[0, pl.ds(k0, tk), cols[g]], kb], axis=1),
                          q2t_ref[g], preferred_element_type=F32) for g in heads]
        probs, alphas = [], []
        for g in heads:
            t = scores[g]
            if first_key is not None:
                t = jnp.where(visible, t, NEG_INF)
            m_old = m_ref[g]
            m_new = jnp.maximum(m_old, jnp.max(t, axis=0, keepdims=True))
            alpha = jnp.exp2(m_old - m_new)
            p = jnp.exp2(t - m_new)
            l_ref[g] = alpha * l_ref[g] + jnp.sum(p, axis=0, keepdims=True)
            m_ref[g] = m_new
            probs.append(p.astype(BF16))
            alphas.append(alpha)
        for g in heads:
            vt = jnp.concatenate([vt_ref[g, blk0 + i] for i in range(blocks_per_tile)], axis=1)
            acc_ref[g] = alphas[g] * acc_ref[g] + jnp.dot(vt, probs[g],
                                                          preferred_element_type=F32)

    n_full = (qi * tq) // tk

    def body(j, carry):
        tile(pl.multiple_of(j * tk, tk), None)
        return carry

    lax.fori_loop(0, n_full, body, 0)
    tail0 = pl.multiple_of(jnp.maximum(qi * tq + tq - tk, 0), tq)
    tile(tail0, n_full * tk)

    lq = lq_ref[...]
    lam = (jnp.exp(jnp.sum(lq[0:1] * lq[1:2], keepdims=True))
           - jnp.exp(jnp.sum(lq[2:3] * lq[3:4], keepdims=True)) + lam_init)
    subln = subln_ref[...]
    for g in heads:
        out_t = acc_ref[g] / l_ref[g]
        a = (out_t[:, :tq] - lam * out_t[:, tq:]).T
        y = _rms(a, subln) * (1.0 - lam_init)
        o_ref[0, :, cols[g]] = y.astype(o_ref.dtype)


def _attention(proj, slopes, lq, subln, lam_init, n_heads):
    B, L, _ = proj.shape
    tq, tk = ATTN_TQ, ATTN_TK
    hd = A_V_DIM
    d_a = n_heads * hd
    kern = functools.partial(_attn_kernel, n_heads=n_heads, lam_init=lam_init)
    return pl.pallas_call(
        kern,
        grid=(B, L // tq),
        in_specs=[
            pl.BlockSpec(memory_space=pltpu.SMEM),
            pl.BlockSpec((4, A_QK_DIM), lambda b, i: (0, 0)),
            pl.BlockSpec((1, hd), lambda b, i: (0, 0)),
            pl.BlockSpec((1, tq, d_a), lambda b, i: (b, i, 0)),
            pl.BlockSpec((1, L, d_a), lambda b, i: (b, 0, 1)),
            pl.BlockSpec((1, L, d_a), lambda b, i: (b, 0, 2), pipeline_mode=pl.Buffered(1)),
        ],
        out_specs=pl.BlockSpec((1, tq, d_a), lambda b, i: (b, i, 0)),
        out_shape=jax.ShapeDtypeStruct((B, L, d_a), BF16),
        scratch_shapes=[pltpu.VMEM((n_heads, L // hd, hd, hd), BF16),
                        pltpu.VMEM((L, hd), BF16),
                        pltpu.VMEM((n_heads, 2 * hd, 2 * tq), BF16),
                        pltpu.VMEM((n_heads, hd, 2 * tq), F32),
                        pltpu.VMEM((n_heads, 1, 2 * tq), F32),
                        pltpu.VMEM((n_heads, 1, 2 * tq), F32)],
        compiler_params=_params("parallel", "arbitrary"),
    )(slopes, lq, subln, proj, proj, proj)


def _ret_kernel(logg_ref, rn_ref, q_ref, k_ref, v_ref, g_ref, o_ref, s_ref,
                *, n_heads, n_chunks):
    @pl.when(pl.program_id(1) == 0)
    def _():
        s_ref[...] = jnp.zeros(s_ref.shape, F32)

    C = CHUNK
    ri = lax.broadcasted_iota(jnp.int32, (C, C), 0)
    ci = lax.broadcasted_iota(jnp.int32, (C, C), 1)
    diff = (ri - ci).astype(F32)
    idx = lax.broadcasted_iota(jnp.int32, (C, 1), 0).astype(F32)
    scale = R_QK_DIM ** -0.5
    rn = rn_ref[...]

    for h in range(n_heads):
        lg = logg_ref[h]
        decay_in = jnp.where(diff >= 0, jnp.exp(lg * jnp.maximum(diff, 0.0)), 0.0) * scale
        k_decay = jnp.exp(lg * (C - 1.0 - idx)) * scale
        q_decay = jnp.exp(lg * (idx + 1.0))
        g_chunk = jnp.exp(jnp.full((1, 1), lg * C, F32))
        qk = slice(h * R_QK_DIM, (h + 1) * R_QK_DIM)
        vv = slice(h * R_V_DIM, (h + 1) * R_V_DIM)
        for c in range(n_chunks):
            rows = slice(c * C, (c + 1) * C)
            q = q_ref[0, rows, qk]
            k = k_ref[0, rows, qk]
            v = v_ref[0, rows, vv]
            s = lax.dot_general(q, k, (((1,), (1,)), ((), ())),
                                preferred_element_type=F32) * decay_in
            intra = jnp.dot(s.astype(BF16), v, preferred_element_type=F32)
            s_prev = s_ref[h]
            q_dec = (q.astype(F32) * q_decay).astype(BF16)
            cross = jnp.dot(q_dec, s_prev.astype(BF16), preferred_element_type=F32)
            k_dec = (k.astype(F32) * k_decay).astype(BF16)
            kv = lax.dot_general(k_dec, v, (((0,), (0,)), ((), ())),
                                 preferred_element_type=F32)
            s_ref[h] = g_chunk * s_prev + kv
            y = _rms(intra + cross, rn)
            gate = g_ref[0, rows, vv].astype(F32)
            o_ref[0, rows, vv] = (y * (gate * jax.nn.sigmoid(gate))).astype(o_ref.dtype)


def _retention(proj, log_g, rn, n_heads, col0):
    B, L, _ = proj.shape
    dqk = n_heads * R_QK_DIM
    dv = n_heads * R_V_DIM
    T = _pick(L, (384, 128))
    kern = functools.partial(_ret_kernel, n_heads=n_heads, n_chunks=T // CHUNK)
    q_blk = col0 // dqk
    v_blk = (col0 + 2 * dqk) // dv
    return pl.pallas_call(
        kern,
        grid=(B, L // T),
        in_specs=[
            pl.BlockSpec(memory_space=pltpu.SMEM),
            pl.BlockSpec((1, R_V_DIM), lambda b, n: (0, 0)),
            pl.BlockSpec((1, T, dqk), lambda b, n: (b, n, q_blk)),
            pl.BlockSpec((1, T, dqk), lambda b, n: (b, n, q_blk + 1)),
            pl.BlockSpec((1, T, dv), lambda b, n: (b, n, v_blk)),
            pl.BlockSpec((1, T, dv), lambda b, n: (b, n, v_blk + 1)),
        ],
        out_specs=pl.BlockSpec((1, T, dv), lambda b, n: (b, n, 0)),
        out_shape=jax.ShapeDtypeStruct((B, L, dv), BF16),
        scratch_shapes=[pltpu.VMEM((n_heads, R_QK_DIM, R_V_DIM), F32)],
        compiler_params=_params("parallel", "arbitrary"),
    )(log_g, rn, proj, proj, proj, proj)


def _out_proj_kernel(ya_ref, yr_ref, w_ref, h_ref, g_ref, ho_ref, u_ref,
                     *, tm, tiles_per_seq):
    da = ya_ref.shape[1]
    acc = jnp.dot(ya_ref[...], w_ref[:da, :], preferred_element_type=F32)
    acc = acc + jnp.dot(yr_ref[...], w_ref[da:, :], preferred_element_type=F32)
    hn = h_ref[...] + acc
    ho_ref[...] = hn
    seq_row0 = (pl.program_id(0) % tiles_per_seq) * tm
    u_ref[...] = _masked_norm(hn, g_ref[...], seq_row0).astype(u_ref.dtype)


def _out_proj(ya, yr, w, h, g, L):
    R, D = h.shape
    da, dr = ya.shape[1], yr.shape[1]
    tm = _pick(L, (528, 384, 128))
    kern = functools.partial(_out_proj_kernel, tm=tm, tiles_per_seq=L // tm)
    return pl.pallas_call(
        kern,
        grid=(R // tm,),
        in_specs=[
            pl.BlockSpec((tm, da), lambda m: (m, 0)),
            pl.BlockSpec((tm, dr), lambda m: (m, 0)),
            pl.BlockSpec((da + dr, D), lambda m: (0, 0)),
            pl.BlockSpec((tm, D), lambda m: (m, 0)),
            pl.BlockSpec((1, D), lambda m: (0, 0)),
        ],
        out_specs=[pl.BlockSpec((tm, D), lambda m: (m, 0)),
                   pl.BlockSpec((tm, D), lambda m: (m, 0))],
        out_shape=[jax.ShapeDtypeStruct((R, D), F32),
                   jax.ShapeDtypeStruct((R, D), BF16)],
        input_output_aliases={3: 0},
        compiler_params=_params("parallel"),
    )(ya, yr, w, h, g)


def _ffn_kernel(u_ref, wg_ref, wu_ref, cw_ref, wd_ref, h_ref, g_ref, *rest,
                tm, tiles_per_seq, final):
    if final:
        ho_ref, gbuf_ref, carry_ref = rest
    else:
        ho_ref, u_out_ref, gbuf_ref, carry_ref = rest
    m = pl.program_id(0)
    f = pl.program_id(1)
    seq_tile = m % tiles_per_seq
    tf = wg_ref.shape[1]

    @pl.when(seq_tile == 0)
    def _():
        gbuf_ref[:CONV_HALO, :] = jnp.zeros((CONV_HALO, tf), F32)

    @pl.when(seq_tile != 0)
    def _():
        gbuf_ref[:CONV_HALO, :] = carry_ref[f]

    @pl.when(f == 0)
    def _():
        ho_ref[...] = h_ref[...]

    u = u_ref[...]
    subs = [slice(c, c + FFN_SUB) for c in range(0, tf, FFN_SUB)]
    gates = [jnp.dot(u, wg_ref[:, cs], preferred_element_type=F32) for cs in subs]
    ups = [jnp.dot(u, wu_ref[:, cs], preferred_element_type=F32) for cs in subs]
    for cs, gate, up in zip(subs, gates, ups):
        gbuf_ref[CONV_HALO:, cs] = gate
        carry_ref[f, :, cs] = gate[tm - CONV_HALO:, :]
        cw = cw_ref[:, cs]
        gc = gate * cw[CONV_W - 1:CONV_W, :]
        for i in range(CONV_W - 1):
            back = CONV_W - 1 - i
            gc = gc + gbuf_ref[pl.ds(CONV_HALO - back, tm), cs] * cw[i:i + 1, :]
        act = (gc * jax.nn.sigmoid(gc)) * up
        ho_ref[...] += jnp.dot(act.astype(BF16), wd_ref[cs, :], preferred_element_type=F32)

    @pl.when(f == pl.num_programs(1) - 1)
    def _():
        hn = ho_ref[...]
        g = g_ref[...]
        if final:
            ho_ref[...] = _rms(hn, g)
        else:
            u_out_ref[...] = _masked_norm(hn, g, seq_tile * tm).astype(u_out_ref.dtype)


def _ffn(u, wg, wu, cw, wd, h, g, L, final):
    R, D = h.shape
    F = wg.shape[1]
    tm = _pick(L, (704, 384, 128))
    tf = _pick(F, (512, 256, 128))
    kern = functools.partial(_ffn_kernel, tm=tm, tiles_per_seq=L // tm, final=final)
    row_spec = pl.BlockSpec((tm, D), lambda m, f: (m, 0))
    out_specs = [row_spec]
    out_shape = [jax.ShapeDtypeStruct((R, D), F32)]
    if not final:
        out_specs.append(row_spec)
        out_shape.append(jax.ShapeDtypeStruct((R, D), BF16))
    return pl.pallas_call(
        kern,
        grid=(R // tm, F // tf),
        in_specs=[
            row_spec,
            pl.BlockSpec((D, tf), lambda m, f: (0, f)),
            pl.BlockSpec((D, tf), lambda m, f: (0, f)),
            pl.BlockSpec((CONV_W, tf), lambda m, f: (0, f)),
            pl.BlockSpec((tf, D), lambda m, f: (f, 0)),
            row_spec,
            pl.BlockSpec((1, D), lambda m, f: (0, 0)),
        ],
        out_specs=out_specs,
        out_shape=out_shape,
        scratch_shapes=[pltpu.VMEM((tm + CONV_HALO, tf), F32),
                        pltpu.VMEM((F // tf, CONV_HALO, tf), F32)],
        input_output_aliases={5: 0},
        compiler_params=_params("arbitrary", "arbitrary"),
    )(u, wg, wu, cw, wd, h, g)


def kernel(x, meta_tokens, attn_norm, w_in, lambda_qk, attn_subln, ret_norm, w_out,
           ffn_norm, w_gate, w_up, conv_w, w_down, final_norm):
    B, S, D = x.shape
    L = CHUNK + S
    R = B * L
    depth = w_in.shape[0]
    a_heads = D // 256
    r_heads = D // 512
    d_a = a_heads * A_V_DIM
    ret_col0 = 2 * a_heads * 2 * A_QK_DIM + d_a

    slopes = 2.0 ** (-8.0 * jnp.arange(1, a_heads + 1, dtype=F32) / a_heads)
    log_g = jnp.log1p(-(2.0 ** (-5.0 - jnp.arange(r_heads, dtype=F32))))

    n_aq = a_heads * 2 * A_QK_DIM
    col_scale = jnp.where(jnp.arange(w_in.shape[2]) < n_aq,
                          LOG2E * A_QK_DIM ** -0.5, 1.0).astype(F32)[None, :]

    w_in_b = w_in.astype(BF16)
    w_out_b = w_out.astype(BF16)
    w_gate_b = w_gate.astype(BF16)
    w_up_b = w_up.astype(BF16)
    w_down_b = w_down.astype(BF16)

    h, u = _prep(x, meta_tokens, attn_norm[0:1])
    h = h.reshape(R, D)
    u = u.reshape(R, D)
    for l in range(depth):
        lam_init = 0.8 - 0.6 * math.exp(-0.3 * l)
        proj = _in_proj(u, w_in_b[l], col_scale).reshape(B, L, -1)
        ya = _attention(proj, slopes, lambda_qk[l], attn_subln[l:l + 1], lam_init, a_heads)
        yr = _retention(proj, log_g, ret_norm[l:l + 1], r_heads, ret_col0)
        h, u2 = _out_proj(ya.reshape(R, -1), yr.reshape(R, -1), w_out_b[l], h,
                          ffn_norm[l:l + 1], L)
        final = l == depth - 1
        g_next = final_norm[None, :] if final else attn_norm[l + 1:l + 2]
        res = _ffn(u2, w_gate_b[l], w_up_b[l], conv_w[l], w_down_b[l], h, g_next, L, final)
        if final:
            h = res[0]
        else:
            h, u = res
    return h.reshape(B, L, D)[:, CHUNK:]
```

```python
import functools
import math

import jax
import jax.numpy as jnp
from jax import lax
from jax.experimental import pallas as pl
from jax.experimental.pallas import tpu as pltpu

N_META = 16
CHUNK = 128
PAD_FRONT = CHUNK - N_META
A_QK_DIM = 64
A_V_DIM = 128
R_QK_DIM = 128
R_V_DIM = 256
CONV_W = 3
EPS = 1e-6
NEG_INF = -1e30
CONV_HALO = 8
FFN_SUB = 256

V7X_VMEM_BYTES = 64 * 1024 * 1024
VMEM_LIMIT_BYTES = V7X_VMEM_BYTES - 8 * 1024 * 1024

F32 = jnp.float32
BF16 = jnp.bfloat16


def _pick(n, candidates):
    for c in candidates:
        if n % c == 0:
            return c
    raise ValueError(f"no tile in {candidates} divides {n}")


def _params(*semantics):
    return pltpu.CompilerParams(dimension_semantics=semantics,
                                vmem_limit_bytes=VMEM_LIMIT_BYTES)


def _rms(xf, g):
    return xf * lax.rsqrt(jnp.mean(xf * xf, axis=-1, keepdims=True) + EPS) * g


def _masked_norm(hn, g, seq_row0):
    row = lax.broadcasted_iota(jnp.int32, hn.shape, 0) + seq_row0
    return jnp.where(row >= PAD_FRONT, _rms(hn, g), 0.0)


def _prep_kernel(x_ref, meta_ref, g_ref, h_ref, u_ref):
    i = pl.program_id(1)
    g = g_ref[...]

    @pl.when(i == 0)
    def _():
        meta = meta_ref[...]
        h_ref[0, :PAD_FRONT, :] = jnp.zeros((PAD_FRONT, meta.shape[1]), F32)
        h_ref[0, PAD_FRONT:, :] = meta
        u_ref[0, :PAD_FRONT, :] = jnp.zeros((PAD_FRONT, meta.shape[1]), BF16)
        u_ref[0, PAD_FRONT:, :] = _rms(meta, g).astype(BF16)

    @pl.when(i > 0)
    def _():
        x = x_ref[0]
        h_ref[0] = x
        u_ref[0] = _rms(x, g).astype(BF16)


def _prep(x, meta, g):
    B, S, D = x.shape
    L = CHUNK + S
    return pl.pallas_call(
        _prep_kernel,
        grid=(B, L // CHUNK),
        in_specs=[
            pl.BlockSpec((1, CHUNK, D), lambda b, i: (b, jnp.maximum(i - 1, 0), 0)),
            pl.BlockSpec((N_META, D), lambda b, i: (0, 0)),
            pl.BlockSpec((1, D), lambda b, i: (0, 0)),
        ],
        out_specs=[
            pl.BlockSpec((1, CHUNK, D), lambda b, i: (b, i, 0)),
            pl.BlockSpec((1, CHUNK, D), lambda b, i: (b, i, 0)),
        ],
        out_shape=[jax.ShapeDtypeStruct((B, L, D), F32),
                   jax.ShapeDtypeStruct((B, L, D), BF16)],
        compiler_params=_params("parallel", "arbitrary"),
    )(x, meta, g)


def _matmul_kernel(x_ref, w_ref, cs_ref, o_ref):
    acc = jnp.dot(x_ref[...], w_ref[...], preferred_element_type=F32)
    o_ref[...] = (acc * cs_ref[...]).astype(o_ref.dtype)


def _in_proj(u, w, col_scale):
    R, D = u.shape
    N = w.shape[1]
    tm = _pick(R, (1408, 1024, 768, 512, 384, 256, 128))
    tn = _pick(N, (1024, 512, 256, 128))
    return pl.pallas_call(
        _matmul_kernel,
        grid=(R // tm, N // tn),
        in_specs=[pl.BlockSpec((tm, D), lambda m, n: (m, 0)),
                  pl.BlockSpec((D, tn), lambda m, n: (0, n)),
                  pl.BlockSpec((1, tn), lambda m, n: (0, n))],
        out_specs=pl.BlockSpec((tm, tn), lambda m, n: (m, n)),
        out_shape=jax.ShapeDtypeStruct((R, N), BF16),
        compiler_params=_params("parallel", "arbitrary"),
    )(u, w, col_scale)


ATTN_TQ = 128
ATTN_TK = 512
BIAS_LANES = 4
LOG2E = math.log2(math.e)


def _attn_kernel(slopes_ref, lq_ref, subln_ref, q_ref, k_ref, v_ref, o_ref,
                 vt_ref, kb_ref, q2t_ref, acc_ref, l_ref, m_ref, *, n_heads, lam_init):
    tq, tk = ATTN_TQ, ATTN_TK
    hd = A_V_DIM
    qi = pl.program_id(1)
    L = k_ref.shape[1]
    heads = range(n_heads)
    cols = [slice(g * hd, (g + 1) * hd) for g in heads]

    @pl.when(qi == 0)
    def _():
        def vt_block(j, carry):
            r0 = pl.multiple_of(j * hd, hd)
            for g in heads:
                vt_ref[g, j] = v_ref[0, pl.ds(r0, hd), cols[g]].astype(F32).T.astype(BF16)
            return carry
        lax.fori_loop(0, L // hd, vt_block, 0)
        kpos = lax.broadcasted_iota(jnp.int32, (L, hd), 0)
        lane = lax.broadcasted_iota(jnp.int32, (L, hd), 1)
        slope = jnp.zeros((L, hd), F32)
        for g in heads:
            slope = jnp.where(lane // BIAS_LANES == g, slopes_ref[g] * LOG2E, slope)
        rest = kpos.astype(F32) * slope
        term = lane % BIAS_LANES
        kb = jnp.where((term == BIAS_LANES - 1) & (lane < n_heads * BIAS_LANES) & (kpos < PAD_FRONT),
                       NEG_INF, 0.0)
        for t in range(BIAS_LANES - 1):
            part = rest.astype(BF16).astype(F32)
            kb = jnp.where(term == t, part, kb)
            rest = rest - part
        kb_ref[...] = kb.astype(BF16)
        row = lax.broadcasted_iota(jnp.int32, (hd, 2 * tq), 0)
        for g in heads:
            q2t_ref[g, hd:, :] = jnp.where(row // BIAS_LANES == g, 1.0, 0.0).astype(BF16)

    dim = lax.broadcasted_iota(jnp.int32, (hd, tq), 0)
    for g in heads:
        qt = q_ref[0, :, cols[g]].astype(F32).T.astype(BF16)
        zero = jnp.zeros_like(qt)
        q2t_ref[g, :hd, :] = jnp.concatenate([jnp.where(dim < A_QK_DIM, qt, zero),
                                              jnp.where(dim >= A_QK_DIM, qt, zero)], axis=1)

    m_ref[...] = jnp.full(m_ref.shape, NEG_INF, F32)
    l_ref[...] = jnp.zeros(l_ref.shape, F32)
    acc_ref[...] = jnp.zeros(acc_ref.shape, F32)

    def tile(k0, width, causal):
        blk0 = k0 // hd
        if causal:
            kpos = lax.broadcasted_iota(jnp.int32, (width, 2 * tq), 0) + k0
            lane = lax.broadcasted_iota(jnp.int32, (width, 2 * tq), 1)
            qpos = jnp.where(lane >= tq, lane - tq, lane) + qi * tq
            visible = kpos <= qpos
        kb = kb_ref[pl.ds(k0, width), :]
        scores = [jnp.dot(jnp.concatenate([k_ref[0, pl.ds(k0, width), cols[g]], kb], axis=1),
                          q2t_ref[g], preferred_element_type=F32) for g in heads]
        probs, alphas = [], []
        for g in heads:
            t = scores[g]
            if causal:
                t = jnp.where(visible, t, NEG_INF)
            m_old = m_ref[g]
            m_new = jnp.maximum(m_old, jnp.max(t, axis=0, keepdims=True))
            alpha = jnp.exp2(m_old - m_new)
            p = jnp.exp2(t - m_new)
            l_ref[g] = alpha * l_ref[g] + jnp.sum(p, axis=0, keepdims=True)
            m_ref[g] = m_new
            probs.append(p.astype(BF16))
            alphas.append(alpha)
        for g in heads:
            vt = jnp.concatenate([vt_ref[g, blk0 + i] for i in range(width // hd)], axis=1)
            acc_ref[g] = alphas[g] * acc_ref[g] + jnp.dot(vt, probs[g],
                                                          preferred_element_type=F32)

    n_full = (qi * tq) // tk

    def body(j, carry):
        tile(pl.multiple_of(j * tk, tk), tk, causal=False)
        return carry

    lax.fori_loop(0, n_full, body, 0)
    tail0 = pl.multiple_of(n_full * tk, tk)
    tiles_left = qi - n_full * (tk // tq)
    for r in range(tk // tq):
        @pl.when(tiles_left == r)
        def _(r=r):
            tile(tail0, (r + 1) * tq, causal=True)

    lq = lq_ref[...]
    lam = (jnp.exp(jnp.sum(lq[0:1] * lq[1:2], keepdims=True))
           - jnp.exp(jnp.sum(lq[2:3] * lq[3:4], keepdims=True)) + lam_init)
    subln = subln_ref[...]
    for g in heads:
        out_t = acc_ref[g] * (1.0 / l_ref[g])
        a = (out_t[:, :tq] - lam * out_t[:, tq:]).T
        y = _rms(a, subln) * (1.0 - lam_init)
        o_ref[0, :, cols[g]] = y.astype(o_ref.dtype)


def _attention(proj, slopes, lq, subln, lam_init, n_heads):
    B, L, _ = proj.shape
    tq, tk = ATTN_TQ, ATTN_TK
    hd = A_V_DIM
    d_a = n_heads * hd
    kern = functools.partial(_attn_kernel, n_heads=n_heads, lam_init=lam_init)
    return pl.pallas_call(
        kern,
        grid=(B, L // tq),
        in_specs=[
            pl.BlockSpec(memory_space=pltpu.SMEM),
            pl.BlockSpec((4, A_QK_DIM), lambda b, i: (0, 0)),
            pl.BlockSpec((1, hd), lambda b, i: (0, 0)),
            pl.BlockSpec((1, tq, d_a), lambda b, i: (b, i, 0)),
            pl.BlockSpec((1, L, d_a), lambda b, i: (b, 0, 1)),
            pl.BlockSpec((1, L, d_a), lambda b, i: (b, 0, 2), pipeline_mode=pl.Buffered(1)),
        ],
        out_specs=pl.BlockSpec((1, tq, d_a), lambda b, i: (b, i, 0)),
        out_shape=jax.ShapeDtypeStruct((B, L, d_a), BF16),
        scratch_shapes=[pltpu.VMEM((n_heads, L // hd, hd, hd), BF16),
                        pltpu.VMEM((L, hd), BF16),
                        pltpu.VMEM((n_heads, 2 * hd, 2 * tq), BF16),
                        pltpu.VMEM((n_heads, hd, 2 * tq), F32),
                        pltpu.VMEM((n_heads, 1, 2 * tq), F32),
                        pltpu.VMEM((n_heads, 1, 2 * tq), F32)],
        compiler_params=_params("parallel", "arbitrary"),
    )(slopes, lq, subln, proj, proj, proj)


def _ret_kernel(logg_ref, rn_ref, q_ref, k_ref, v_ref, g_ref, o_ref, s_ref,
                *, n_heads, n_chunks):
    @pl.when(pl.program_id(1) == 0)
    def _():
        s_ref[...] = jnp.zeros(s_ref.shape, F32)

    C = CHUNK
    ri = lax.broadcasted_iota(jnp.int32, (C, C), 0)
    ci = lax.broadcasted_iota(jnp.int32, (C, C), 1)
    diff = (ri - ci).astype(F32)
    idx = lax.broadcasted_iota(jnp.int32, (C, 1), 0).astype(F32)
    scale = R_QK_DIM ** -0.5
    rn = rn_ref[...]

    for h in range(n_heads):
        lg = logg_ref[h]
        decay_in = jnp.where(diff >= 0, jnp.exp(lg * jnp.maximum(diff, 0.0)), 0.0) * scale
        k_decay = jnp.exp(lg * (C - 1.0 - idx)) * scale
        q_decay = jnp.exp(lg * (idx + 1.0))
        g_chunk = jnp.exp(jnp.full((1, 1), lg * C, F32))
        qk = slice(h * R_QK_DIM, (h + 1) * R_QK_DIM)
        vv = slice(h * R_V_DIM, (h + 1) * R_V_DIM)
        for c in range(n_chunks):
            rows = slice(c * C, (c + 1) * C)
            q = q_ref[0, rows, qk]
            k = k_ref[0, rows, qk]
            v = v_ref[0, rows, vv]
            s = lax.dot_general(q, k, (((1,), (1,)), ((), ())),
                                preferred_element_type=F32) * decay_in
            intra = jnp.dot(s.astype(BF16), v, preferred_element_type=F32)
            s_prev = s_ref[h]
            q_dec = (q.astype(F32) * q_decay).astype(BF16)
            cross = jnp.dot(q_dec, s_prev.astype(BF16), preferred_element_type=F32)
            k_dec = (k.astype(F32) * k_decay).astype(BF16)
            kv = lax.dot_general(k_dec, v, (((0,), (0,)), ((), ())),
                                 preferred_element_type=F32)
            s_ref[h] = g_chunk * s_prev + kv
            y = _rms(intra + cross, rn)
            gate = g_ref[0, rows, vv].astype(F32)
            o_ref[0, rows, vv] = (y * (gate * jax.nn.sigmoid(gate))).astype(o_ref.dtype)


def _retention(proj, log_g, rn, n_heads, col0):
    B, L, _ = proj.shape
    dqk = n_heads * R_QK_DIM
    dv = n_heads * R_V_DIM
    T = _pick(L, (384, 128))
    kern = functools.partial(_ret_kernel, n_heads=n_heads, n_chunks=T // CHUNK)
    q_blk = col0 // dqk
    v_blk = (col0 + 2 * dqk) // dv
    return pl.pallas_call(
        kern,
        grid=(B, L // T),
        in_specs=[
            pl.BlockSpec(memory_space=pltpu.SMEM),
            pl.BlockSpec((1, R_V_DIM), lambda b, n: (0, 0)),
            pl.BlockSpec((1, T, dqk), lambda b, n: (b, n, q_blk)),
            pl.BlockSpec((1, T, dqk), lambda b, n: (b, n, q_blk + 1)),
            pl.BlockSpec((1, T, dv), lambda b, n: (b, n, v_blk)),
            pl.BlockSpec((1, T, dv), lambda b, n: (b, n, v_blk + 1)),
        ],
        out_specs=pl.BlockSpec((1, T, dv), lambda b, n: (b, n, 0)),
        out_shape=jax.ShapeDtypeStruct((B, L, dv), BF16),
        scratch_shapes=[pltpu.VMEM((n_heads, R_QK_DIM, R_V_DIM), F32)],
        compiler_params=_params("parallel", "arbitrary"),
    )(log_g, rn, proj, proj, proj, proj)


def _out_proj_kernel(ya_ref, yr_ref, w_ref, h_ref, g_ref, ho_ref, u_ref,
                     *, tm, tiles_per_seq):
    da = ya_ref.shape[1]
    acc = jnp.dot(ya_ref[...], w_ref[:da, :], preferred_element_type=F32)
    acc = acc + jnp.dot(yr_ref[...], w_ref[da:, :], preferred_element_type=F32)
    hn = h_ref[...] + acc
    ho_ref[...] = hn
    seq_row0 = (pl.program_id(0) % tiles_per_seq) * tm
    u_ref[...] = _masked_norm(hn, g_ref[...], seq_row0).astype(u_ref.dtype)


def _out_proj(ya, yr, w, h, g, L):
    R, D = h.shape
    da, dr = ya.shape[1], yr.shape[1]
    tm = _pick(L, (528, 384, 128))
    kern = functools.partial(_out_proj_kernel, tm=tm, tiles_per_seq=L // tm)
    return pl.pallas_call(
        kern,
        grid=(R // tm,),
        in_specs=[
            pl.BlockSpec((tm, da), lambda m: (m, 0)),
            pl.BlockSpec((tm, dr), lambda m: (m, 0)),
            pl.BlockSpec((da + dr, D), lambda m: (0, 0)),
            pl.BlockSpec((tm, D), lambda m: (m, 0)),
            pl.BlockSpec((1, D), lambda m: (0, 0)),
        ],
        out_specs=[pl.BlockSpec((tm, D), lambda m: (m, 0)),
                   pl.BlockSpec((tm, D), lambda m: (m, 0))],
        out_shape=[jax.ShapeDtypeStruct((R, D), F32),
                   jax.ShapeDtypeStruct((R, D), BF16)],
        input_output_aliases={3: 0},
        compiler_params=_params("parallel"),
    )(ya, yr, w, h, g)


def _ffn_kernel(u_ref, wg_ref, wu_ref, cw_ref, wd_ref, h_ref, g_ref, *rest,
                tm, tiles_per_seq, final):
    if final:
        ho_ref, gbuf_ref, carry_ref = rest
    else:
        ho_ref, u_out_ref, gbuf_ref, carry_ref = rest
    m = pl.program_id(0)
    f = pl.program_id(1)
    seq_tile = m % tiles_per_seq
    tf = wg_ref.shape[1]

    @pl.when(seq_tile == 0)
    def _():
        gbuf_ref[:CONV_HALO, :] = jnp.zeros((CONV_HALO, tf), F32)

    @pl.when(seq_tile != 0)
    def _():
        gbuf_ref[:CONV_HALO, :] = carry_ref[f]

    @pl.when(f == 0)
    def _():
        ho_ref[...] = h_ref[...]

    u = u_ref[...]
    subs = [slice(c, c + FFN_SUB) for c in range(0, tf, FFN_SUB)]
    gates = [jnp.dot(u, wg_ref[:, cs], preferred_element_type=F32) for cs in subs]
    ups = [jnp.dot(u, wu_ref[:, cs], preferred_element_type=F32) for cs in subs]
    for cs, gate, up in zip(subs, gates, ups):
        gbuf_ref[CONV_HALO:, cs] = gate
        carry_ref[f, :, cs] = gate[tm - CONV_HALO:, :]
        cw = cw_ref[:, cs]
        gc = gate * cw[CONV_W - 1:CONV_W, :]
        for i in range(CONV_W - 1):
            back = CONV_W - 1 - i
            gc = gc + gbuf_ref[pl.ds(CONV_HALO - back, tm), cs] * cw[i:i + 1, :]
        act = (gc * jax.nn.sigmoid(gc)) * up
        ho_ref[...] += jnp.dot(act.astype(BF16), wd_ref[cs, :], preferred_element_type=F32)

    @pl.when(f == pl.num_programs(1) - 1)
    def _():
        hn = ho_ref[...]
        g = g_ref[...]
        if final:
            ho_ref[...] = _rms(hn, g)
        else:
            u_out_ref[...] = _masked_norm(hn, g, seq_tile * tm).astype(u_out_ref.dtype)


def _ffn(u, wg, wu, cw, wd, h, g, L, final):
    R, D = h.shape
    F = wg.shape[1]
    tm = _pick(L, (704, 384, 128))
    tf = _pick(F, (512, 256, 128))
    kern = functools.partial(_ffn_kernel, tm=tm, tiles_per_seq=L // tm, final=final)
    row_spec = pl.BlockSpec((tm, D), lambda m, f: (m, 0))
    out_specs = [row_spec]
    out_shape = [jax.ShapeDtypeStruct((R, D), F32)]
    if not final:
        out_specs.append(row_spec)
        out_shape.append(jax.ShapeDtypeStruct((R, D), BF16))
    return pl.pallas_call(
        kern,
        grid=(R // tm, F // tf),
        in_specs=[
            row_spec,
            pl.BlockSpec((D, tf), lambda m, f: (0, f)),
            pl.BlockSpec((D, tf), lambda m, f: (0, f)),
            pl.BlockSpec((CONV_W, tf), lambda m, f: (0, f)),
            pl.BlockSpec((tf, D), lambda m, f: (f, 0)),
            row_spec,
            pl.BlockSpec((1, D), lambda m, f: (0, 0)),
        ],
        out_specs=out_specs,
        out_shape=out_shape,
        scratch_shapes=[pltpu.VMEM((tm + CONV_HALO, tf), F32),
                        pltpu.VMEM((F // tf, CONV_HALO, tf), F32)],
        input_output_aliases={5: 0},
        compiler_params=_params("arbitrary", "arbitrary"),
    )(u, wg, wu, cw, wd, h, g)


def kernel(x, meta_tokens, attn_norm, w_in, lambda_qk, attn_subln, ret_norm, w_out,
           ffn_norm, w_gate, w_up, conv_w, w_down, final_norm):
    B, S, D = x.shape
    L = CHUNK + S
    R = B * L
    depth = w_in.shape[0]
    a_heads = D // 256
    r_heads = D // 512
    d_a = a_heads * A_V_DIM
    ret_col0 = 2 * a_heads * 2 * A_QK_DIM + d_a

    slopes = 2.0 ** (-8.0 * jnp.arange(1, a_heads + 1, dtype=F32) / a_heads)
    log_g = jnp.log1p(-(2.0 ** (-5.0 - jnp.arange(r_heads, dtype=F32))))

    n_aq = a_heads * 2 * A_QK_DIM
    col_scale = jnp.where(jnp.arange(w_in.shape[2]) < n_aq,
                          LOG2E * A_QK_DIM ** -0.5, 1.0).astype(F32)[None, :]

    w_in_b = w_in.astype(BF16)
    w_out_b = w_out.astype(BF16)
    w_gate_b = w_gate.astype(BF16)
    w_up_b = w_up.astype(BF16)
    w_down_b = w_down.astype(BF16)

    h, u = _prep(x, meta_tokens, attn_norm[0:1])
    h = h.reshape(R, D)
    u = u.reshape(R, D)
    for l in range(depth):
        lam_init = 0.8 - 0.6 * math.exp(-0.3 * l)
        proj = _in_proj(u, w_in_b[l], col_scale).reshape(B, L, -1)
        ya = _attention(proj, slopes, lambda_qk[l], attn_subln[l:l + 1], lam_init, a_heads)
        yr = _retention(proj, log_g, ret_norm[l:l + 1], r_heads, ret_col0)
        h, u2 = _out_proj(ya.reshape(R, -1), yr.reshape(R, -1), w_out_b[l], h,
                          ffn_norm[l:l + 1], L)
        final = l == depth - 1
        g_next = final_norm[None, :] if final else attn_norm[l + 1:l + 2]
        res = _ffn(u2, w_gate_b[l], w_up_b[l], conv_w[l], w_down_b[l], h, g_next, L, final)
        if final:
            h = res[0]
        else:
            h, u = res
    return h.reshape(B, L, D)[:, CHUNK:]
```

```python
import functools
import math

import jax
import jax.numpy as jnp
from jax import lax
from jax.experimental import pallas as pl
from jax.experimental.pallas import tpu as pltpu

N_META = 16
CHUNK = 128
PAD_FRONT = CHUNK - N_META
A_QK_DIM = 64
A_V_DIM = 128
R_QK_DIM = 128
R_V_DIM = 256
CONV_W = 3
EPS = 1e-6
NEG_INF = -1e30
CONV_HALO = 8
FFN_SUB = 256

V7X_VMEM_BYTES = 64 * 1024 * 1024
VMEM_LIMIT_BYTES = V7X_VMEM_BYTES - 8 * 1024 * 1024

F32 = jnp.float32
BF16 = jnp.bfloat16


def _pick(n, candidates):
    for c in candidates:
        if n % c == 0:
            return c
    raise ValueError(f"no tile in {candidates} divides {n}")


def _params(*semantics):
    return pltpu.CompilerParams(dimension_semantics=semantics,
                                vmem_limit_bytes=VMEM_LIMIT_BYTES)


def _rms(xf, g):
    return xf * lax.rsqrt(jnp.mean(xf * xf, axis=-1, keepdims=True) + EPS) * g


def _masked_norm(hn, g, seq_row0):
    row = lax.broadcasted_iota(jnp.int32, hn.shape, 0) + seq_row0
    return jnp.where(row >= PAD_FRONT, _rms(hn, g), 0.0)


def _prep_kernel(x_ref, meta_ref, g_ref, h_ref, u_ref):
    i = pl.program_id(1)
    g = g_ref[...]

    @pl.when(i == 0)
    def _():
        meta = meta_ref[...]
        h_ref[0, :PAD_FRONT, :] = jnp.zeros((PAD_FRONT, meta.shape[1]), F32)
        h_ref[0, PAD_FRONT:, :] = meta
        u_ref[0, :PAD_FRONT, :] = jnp.zeros((PAD_FRONT, meta.shape[1]), BF16)
        u_ref[0, PAD_FRONT:, :] = _rms(meta, g).astype(BF16)

    @pl.when(i > 0)
    def _():
        x = x_ref[0]
        h_ref[0] = x
        u_ref[0] = _rms(x, g).astype(BF16)


def _prep(x, meta, g):
    B, S, D = x.shape
    L = CHUNK + S
    return pl.pallas_call(
        _prep_kernel,
        grid=(B, L // CHUNK),
        in_specs=[
            pl.BlockSpec((1, CHUNK, D), lambda b, i: (b, jnp.maximum(i - 1, 0), 0)),
            pl.BlockSpec((N_META, D), lambda b, i: (0, 0)),
            pl.BlockSpec((1, D), lambda b, i: (0, 0)),
        ],
        out_specs=[
            pl.BlockSpec((1, CHUNK, D), lambda b, i: (b, i, 0)),
            pl.BlockSpec((1, CHUNK, D), lambda b, i: (b, i, 0)),
        ],
        out_shape=[jax.ShapeDtypeStruct((B, L, D), F32),
                   jax.ShapeDtypeStruct((B, L, D), BF16)],
        compiler_params=_params("parallel", "arbitrary"),
    )(x, meta, g)


def _matmul_kernel(x_ref, w_ref, cs_ref, *rest):
    n_cast = (len(rest) - 1) // 2
    o_ref = rest[n_cast]
    acc = jnp.dot(x_ref[...], w_ref[...], preferred_element_type=F32)
    o_ref[...] = (acc * cs_ref[...]).astype(o_ref.dtype)
    for src, dst in zip(rest[:n_cast], rest[n_cast + 1:]):
        dst[...] = src[0].astype(dst.dtype)


def _cast_block_rows(rows, steps):
    for br in range(16, rows + 1, 16):
        if rows % br == 0 and rows // br <= steps:
            return br
    raise ValueError(f"cannot split {rows} rows over {steps} steps")


def _in_proj(u, w, col_scale, layer, f32_weights):
    R, D = u.shape
    N = w.shape[1]
    tm = _pick(R, (1408, 1024, 768, 512, 384, 256, 128))
    tn = _pick(N, (1024, 512, 256, 128))
    n_tiles = N // tn
    steps = (R // tm) * n_tiles
    in_specs = [pl.BlockSpec((tm, D), lambda m, n: (m, 0)),
                pl.BlockSpec((D, tn), lambda m, n: (0, n)),
                pl.BlockSpec((1, tn), lambda m, n: (0, n))]
    out_specs = [pl.BlockSpec((tm, tn), lambda m, n: (m, n))]
    out_shape = [jax.ShapeDtypeStruct((R, N), BF16)]
    for wf in f32_weights:
        _, rows, cols = wf.shape
        br = _cast_block_rows(rows, steps)
        last = rows // br - 1
        in_specs.append(pl.BlockSpec(
            (1, br, cols), lambda m, n, last=last: (layer, jnp.minimum(m * n_tiles + n, last), 0)))
        out_specs.append(pl.BlockSpec(
            (br, cols), lambda m, n, last=last: (jnp.minimum(m * n_tiles + n, last), 0)))
        out_shape.append(jax.ShapeDtypeStruct((rows, cols), BF16))
    return pl.pallas_call(
        _matmul_kernel,
        grid=(R // tm, n_tiles),
        in_specs=in_specs,
        out_specs=out_specs,
        out_shape=out_shape,
        compiler_params=_params("arbitrary", "arbitrary"),
    )(u, w, col_scale, *f32_weights)


ATTN_TQ = 128
ATTN_TK = 512
BIAS_LANES = 4
LOG2E = math.log2(math.e)


def _attn_kernel(slopes_ref, lq_ref, subln_ref, q_ref, k_ref, v_ref, o_ref,
                 vt_ref, kb_ref, q2t_ref, acc_ref, l_ref, m_ref, *, n_heads, lam_init):
    tq, tk = ATTN_TQ, ATTN_TK
    hd = A_V_DIM
    qi = pl.program_id(1)
    L = k_ref.shape[1]
    heads = range(n_heads)
    cols = [slice(g * hd, (g + 1) * hd) for g in heads]

    @pl.when(qi == 0)
    def _():
        def vt_block(j, carry):
            r0 = pl.multiple_of(j * hd, hd)
            for g in heads:
                vt_ref[g, j] = v_ref[0, pl.ds(r0, hd), cols[g]].astype(F32).T.astype(BF16)
            return carry
        lax.fori_loop(0, L // hd, vt_block, 0)
        kpos = lax.broadcasted_iota(jnp.int32, (L, hd), 0)
        lane = lax.broadcasted_iota(jnp.int32, (L, hd), 1)
        slope = jnp.zeros((L, hd), F32)
        for g in heads:
            slope = jnp.where(lane // BIAS_LANES == g, slopes_ref[g] * LOG2E, slope)
        rest = kpos.astype(F32) * slope
        term = lane % BIAS_LANES
        kb = jnp.where((term == BIAS_LANES - 1) & (lane < n_heads * BIAS_LANES) & (kpos < PAD_FRONT),
                       NEG_INF, 0.0)
        for t in range(BIAS_LANES - 1):
            part = rest.astype(BF16).astype(F32)
            kb = jnp.where(term == t, part, kb)
            rest = rest - part
        kb_ref[...] = kb.astype(BF16)
        row = lax.broadcasted_iota(jnp.int32, (hd, 2 * tq), 0)
        for g in heads:
            q2t_ref[g, hd:, :] = jnp.where(row // BIAS_LANES == g, 1.0, 0.0).astype(BF16)

    dim = lax.broadcasted_iota(jnp.int32, (hd, tq), 0)
    for g in heads:
        qt = q_ref[0, :, cols[g]].astype(F32).T.astype(BF16)
        zero = jnp.zeros_like(qt)
        q2t_ref[g, :hd, :] = jnp.concatenate([jnp.where(dim < A_QK_DIM, qt, zero),
                                              jnp.where(dim >= A_QK_DIM, qt, zero)], axis=1)

    m_ref[...] = jnp.full(m_ref.shape, NEG_INF, F32)
    l_ref[...] = jnp.zeros(l_ref.shape, F32)
    acc_ref[...] = jnp.zeros(acc_ref.shape, F32)

    def tile(k0, width, causal):
        blk0 = k0 // hd
        if causal:
            kpos = lax.broadcasted_iota(jnp.int32, (width, 2 * tq), 0) + k0
            lane = lax.broadcasted_iota(jnp.int32, (width, 2 * tq), 1)
            qpos = jnp.where(lane >= tq, lane - tq, lane) + qi * tq
            visible = kpos <= qpos
        kb = kb_ref[pl.ds(k0, width), :]
        scores = [jnp.dot(jnp.concatenate([k_ref[0, pl.ds(k0, width), cols[g]], kb], axis=1),
                          q2t_ref[g], preferred_element_type=F32) for g in heads]
        probs, alphas = [], []
        for g in heads:
            t = scores[g]
            if causal:
                t = jnp.where(visible, t, NEG_INF)
            m_old = m_ref[g]
            m_new = jnp.maximum(m_old, jnp.max(t, axis=0, keepdims=True))
            alpha = jnp.exp2(m_old - m_new)
            p = jnp.exp2(t - m_new)
            l_ref[g] = alpha * l_ref[g] + jnp.sum(p, axis=0, keepdims=True)
            m_ref[g] = m_new
            probs.append(p.astype(BF16))
            alphas.append(alpha)
        for g in heads:
            vt = jnp.concatenate([vt_ref[g, blk0 + i] for i in range(width // hd)], axis=1)
            acc_ref[g] = alphas[g] * acc_ref[g] + jnp.dot(vt, probs[g],
                                                          preferred_element_type=F32)

    n_full = (qi * tq) // tk

    def body(j, carry):
        tile(pl.multiple_of(j * tk, tk), tk, causal=False)
        return carry

    lax.fori_loop(0, n_full, body, 0)
    tail0 = pl.multiple_of(n_full * tk, tk)
    tiles_left = qi - n_full * (tk // tq)
    for r in range(tk // tq):
        @pl.when(tiles_left == r)
        def _(r=r):
            tile(tail0, (r + 1) * tq, causal=True)

    lq = lq_ref[...]
    lam = (jnp.exp(jnp.sum(lq[0:1] * lq[1:2], keepdims=True))
           - jnp.exp(jnp.sum(lq[2:3] * lq[3:4], keepdims=True)) + lam_init)
    subln = subln_ref[...]
    for g in heads:
        out_t = acc_ref[g] * (1.0 / l_ref[g])
        a = (out_t[:, :tq] - lam * out_t[:, tq:]).T
        y = _rms(a, subln) * (1.0 - lam_init)
        o_ref[0, :, cols[g]] = y.astype(o_ref.dtype)


def _attention(proj, slopes, lq, subln, lam_init, n_heads):
    B, L, _ = proj.shape
    tq, tk = ATTN_TQ, ATTN_TK
    hd = A_V_DIM
    d_a = n_heads * hd
    kern = functools.partial(_attn_kernel, n_heads=n_heads, lam_init=lam_init)
    return pl.pallas_call(
        kern,
        grid=(B, L // tq),
        in_specs=[
            pl.BlockSpec(memory_space=pltpu.SMEM),
            pl.BlockSpec((4, A_QK_DIM), lambda b, i: (0, 0)),
            pl.BlockSpec((1, hd), lambda b, i: (0, 0)),
            pl.BlockSpec((1, tq, d_a), lambda b, i: (b, i, 0)),
            pl.BlockSpec((1, L, d_a), lambda b, i: (b, 0, 1)),
            pl.BlockSpec((1, L, d_a), lambda b, i: (b, 0, 2), pipeline_mode=pl.Buffered(1)),
        ],
        out_specs=pl.BlockSpec((1, tq, d_a), lambda b, i: (b, i, 0)),
        out_shape=jax.ShapeDtypeStruct((B, L, d_a), BF16),
        scratch_shapes=[pltpu.VMEM((n_heads, L // hd, hd, hd), BF16),
                        pltpu.VMEM((L, hd), BF16),
                        pltpu.VMEM((n_heads, 2 * hd, 2 * tq), BF16),
                        pltpu.VMEM((n_heads, hd, 2 * tq), F32),
                        pltpu.VMEM((n_heads, 1, 2 * tq), F32),
                        pltpu.VMEM((n_heads, 1, 2 * tq), F32)],
        compiler_params=_params("parallel", "arbitrary"),
    )(slopes, lq, subln, proj, proj, proj)


def _ret_kernel(logg_ref, rn_ref, q_ref, k_ref, v_ref, g_ref, o_ref, s_ref,
                *, n_heads, n_chunks):
    @pl.when(pl.program_id(1) == 0)
    def _():
        s_ref[...] = jnp.zeros(s_ref.shape, F32)

    C = CHUNK
    ri = lax.broadcasted_iota(jnp.int32, (C, C), 0)
    ci = lax.broadcasted_iota(jnp.int32, (C, C), 1)
    diff = (ri - ci).astype(F32)
    idx = lax.broadcasted_iota(jnp.int32, (C, 1), 0).astype(F32)
    scale = R_QK_DIM ** -0.5
    rn = rn_ref[...]

    for h in range(n_heads):
        lg = logg_ref[h]
        decay_in = jnp.where(diff >= 0, jnp.exp(lg * jnp.maximum(diff, 0.0)), 0.0) * scale
        k_decay = jnp.exp(lg * (C - 1.0 - idx)) * scale
        q_decay = jnp.exp(lg * (idx + 1.0))
        g_chunk = jnp.exp(jnp.full((1, 1), lg * C, F32))
        qk = slice(h * R_QK_DIM, (h + 1) * R_QK_DIM)
        vv = slice(h * R_V_DIM, (h + 1) * R_V_DIM)
        for c in range(n_chunks):
            rows = slice(c * C, (c + 1) * C)
            q = q_ref[0, rows, qk]
            k = k_ref[0, rows, qk]
            v = v_ref[0, rows, vv]
            s = lax.dot_general(q, k, (((1,), (1,)), ((), ())),
                                preferred_element_type=F32) * decay_in
            intra = jnp.dot(s.astype(BF16), v, preferred_element_type=F32)
            s_prev = s_ref[h]
            q_dec = (q.astype(F32) * q_decay).astype(BF16)
            cross = jnp.dot(q_dec, s_prev.astype(BF16), preferred_element_type=F32)
            k_dec = (k.astype(F32) * k_decay).astype(BF16)
            kv = lax.dot_general(k_dec, v, (((0,), (0,)), ((), ())),
                                 preferred_element_type=F32)
            s_ref[h] = g_chunk * s_prev + kv
            y = _rms(intra + cross, rn)
            gate = g_ref[0, rows, vv].astype(F32)
            o_ref[0, rows, vv] = (y * (gate * jax.nn.sigmoid(gate))).astype(o_ref.dtype)


def _retention(proj, log_g, rn, n_heads, col0):
    B, L, _ = proj.shape
    dqk = n_heads * R_QK_DIM
    dv = n_heads * R_V_DIM
    T = _pick(L, (384, 128))
    kern = functools.partial(_ret_kernel, n_heads=n_heads, n_chunks=T // CHUNK)
    q_blk = col0 // dqk
    v_blk = (col0 + 2 * dqk) // dv
    return pl.pallas_call(
        kern,
        grid=(B, L // T),
        in_specs=[
            pl.BlockSpec(memory_space=pltpu.SMEM),
            pl.BlockSpec((1, R_V_DIM), lambda b, n: (0, 0)),
            pl.BlockSpec((1, T, dqk), lambda b, n: (b, n, q_blk)),
            pl.BlockSpec((1, T, dqk), lambda b, n: (b, n, q_blk + 1)),
            pl.BlockSpec((1, T, dv), lambda b, n: (b, n, v_blk)),
            pl.BlockSpec((1, T, dv), lambda b, n: (b, n, v_blk + 1)),
        ],
        out_specs=pl.BlockSpec((1, T, dv), lambda b, n: (b, n, 0)),
        out_shape=jax.ShapeDtypeStruct((B, L, dv), BF16),
        scratch_shapes=[pltpu.VMEM((n_heads, R_QK_DIM, R_V_DIM), F32)],
        compiler_params=_params("parallel", "arbitrary"),
    )(log_g, rn, proj, proj, proj, proj)


def _out_proj_kernel(ya_ref, yr_ref, w_ref, h_ref, g_ref, ho_ref, u_ref,
                     *, tm, tiles_per_seq):
    da = ya_ref.shape[1]
    acc = jnp.dot(ya_ref[...], w_ref[:da, :], preferred_element_type=F32)
    acc = acc + jnp.dot(yr_ref[...], w_ref[da:, :], preferred_element_type=F32)
    hn = h_ref[...] + acc
    ho_ref[...] = hn
    seq_row0 = (pl.program_id(0) % tiles_per_seq) * tm
    u_ref[...] = _masked_norm(hn, g_ref[...], seq_row0).astype(u_ref.dtype)


def _out_proj(ya, yr, w, h, g, L):
    R, D = h.shape
    da, dr = ya.shape[1], yr.shape[1]
    tm = _pick(L, (528, 384, 128))
    kern = functools.partial(_out_proj_kernel, tm=tm, tiles_per_seq=L // tm)
    return pl.pallas_call(
        kern,
        grid=(R // tm,),
        in_specs=[
            pl.BlockSpec((tm, da), lambda m: (m, 0)),
            pl.BlockSpec((tm, dr), lambda m: (m, 0)),
            pl.BlockSpec((da + dr, D), lambda m: (0, 0)),
            pl.BlockSpec((tm, D), lambda m: (m, 0)),
            pl.BlockSpec((1, D), lambda m: (0, 0)),
        ],
        out_specs=[pl.BlockSpec((tm, D), lambda m: (m, 0)),
                   pl.BlockSpec((tm, D), lambda m: (m, 0))],
        out_shape=[jax.ShapeDtypeStruct((R, D), F32),
                   jax.ShapeDtypeStruct((R, D), BF16)],
        input_output_aliases={3: 0},
        compiler_params=_params("parallel"),
    )(ya, yr, w, h, g)


def _ffn_kernel(u_ref, wg_ref, wu_ref, cw_ref, wd_ref, h_ref, g_ref, *rest,
                tm, tiles_per_seq, final):
    if final:
        ho_ref, gbuf_ref, carry_ref = rest
    else:
        ho_ref, u_out_ref, gbuf_ref, carry_ref = rest
    m = pl.program_id(0)
    f = pl.program_id(1)
    seq_tile = m % tiles_per_seq
    tf = wg_ref.shape[1]

    @pl.when(seq_tile == 0)
    def _():
        gbuf_ref[:CONV_HALO, :] = jnp.zeros((CONV_HALO, tf), F32)

    @pl.when(seq_tile != 0)
    def _():
        gbuf_ref[:CONV_HALO, :] = carry_ref[f]

    @pl.when(f == 0)
    def _():
        ho_ref[...] = h_ref[...]

    u = u_ref[...]
    subs = [slice(c, c + FFN_SUB) for c in range(0, tf, FFN_SUB)]
    gates = [jnp.dot(u, wg_ref[:, cs], preferred_element_type=F32) for cs in subs]
    ups = [jnp.dot(u, wu_ref[:, cs], preferred_element_type=F32) for cs in subs]
    for cs, gate, up in zip(subs, gates, ups):
        gbuf_ref[CONV_HALO:, cs] = gate
        carry_ref[f, :, cs] = gate[tm - CONV_HALO:, :]
        cw = cw_ref[:, cs]
        gc = gate * cw[CONV_W - 1:CONV_W, :]
        for i in range(CONV_W - 1):
            back = CONV_W - 1 - i
            gc = gc + gbuf_ref[pl.ds(CONV_HALO - back, tm), cs] * cw[i:i + 1, :]
        act = (gc * jax.nn.sigmoid(gc)) * up
        ho_ref[...] += jnp.dot(act.astype(BF16), wd_ref[cs, :], preferred_element_type=F32)

    @pl.when(f == pl.num_programs(1) - 1)
    def _():
        hn = ho_ref[...]
        g = g_ref[...]
        if final:
            ho_ref[...] = _rms(hn, g)
        else:
            u_out_ref[...] = _masked_norm(hn, g, seq_tile * tm).astype(u_out_ref.dtype)


def _ffn(u, wg, wu, cw, wd, h, g, L, final):
    R, D = h.shape
    F = wg.shape[1]
    tm = _pick(L, (704, 384, 128))
    tf = _pick(F, (512, 256, 128))
    kern = functools.partial(_ffn_kernel, tm=tm, tiles_per_seq=L // tm, final=final)
    row_spec = pl.BlockSpec((tm, D), lambda m, f: (m, 0))
    out_specs = [row_spec]
    out_shape = [jax.ShapeDtypeStruct((R, D), F32)]
    if not final:
        out_specs.append(row_spec)
        out_shape.append(jax.ShapeDtypeStruct((R, D), BF16))
    return pl.pallas_call(
        kern,
        grid=(R // tm, F // tf),
        in_specs=[
            row_spec,
            pl.BlockSpec((D, tf), lambda m, f: (0, f)),
            pl.BlockSpec((D, tf), lambda m, f: (0, f)),
            pl.BlockSpec((CONV_W, tf), lambda m, f: (0, f)),
            pl.BlockSpec((tf, D), lambda m, f: (f, 0)),
            row_spec,
            pl.BlockSpec((1, D), lambda m, f: (0, 0)),
        ],
        out_specs=out_specs,
        out_shape=out_shape,
        scratch_shapes=[pltpu.VMEM((tm + CONV_HALO, tf), F32),
                        pltpu.VMEM((F // tf, CONV_HALO, tf), F32)],
        input_output_aliases={5: 0},
        compiler_params=_params("arbitrary", "arbitrary"),
    )(u, wg, wu, cw, wd, h, g)


def kernel(x, meta_tokens, attn_norm, w_in, lambda_qk, attn_subln, ret_norm, w_out,
           ffn_norm, w_gate, w_up, conv_w, w_down, final_norm):
    B, S, D = x.shape
    L = CHUNK + S
    R = B * L
    depth = w_in.shape[0]
    a_heads = D // 256
    r_heads = D // 512
    d_a = a_heads * A_V_DIM
    ret_col0 = 2 * a_heads * 2 * A_QK_DIM + d_a

    slopes = 2.0 ** (-8.0 * jnp.arange(1, a_heads + 1, dtype=F32) / a_heads)
    log_g = jnp.log1p(-(2.0 ** (-5.0 - jnp.arange(r_heads, dtype=F32))))

    n_aq = a_heads * 2 * A_QK_DIM
    col_scale = jnp.where(jnp.arange(w_in.shape[2]) < n_aq,
                          LOG2E * A_QK_DIM ** -0.5, 1.0).astype(F32)[None, :]

    w_in_b = w_in.astype(BF16)
    w_out_b = w_out.astype(BF16)

    h, u = _prep(x, meta_tokens, attn_norm[0:1])
    h = h.reshape(R, D)
    u = u.reshape(R, D)
    for l in range(depth):
        lam_init = 0.8 - 0.6 * math.exp(-0.3 * l)
        proj, w_gate_b, w_up_b, w_down_b = _in_proj(u, w_in_b[l], col_scale, l,
                                                    (w_gate, w_up, w_down))
        proj = proj.reshape(B, L, -1)
        ya = _attention(proj, slopes, lambda_qk[l], attn_subln[l:l + 1], lam_init, a_heads)
        yr = _retention(proj, log_g, ret_norm[l:l + 1], r_heads, ret_col0)
        h, u2 = _out_proj(ya.reshape(R, -1), yr.reshape(R, -1), w_out_b[l], h,
                          ffn_norm[l:l + 1], L)
        final = l == depth - 1
        g_next = final_norm[None, :] if final else attn_norm[l + 1:l + 2]
        res = _ffn(u2, w_gate_b, w_up_b, conv_w[l], w_down_b, h, g_next, L, final)
        if final:
            h = res[0]
        else:
            h, u = res
    return h.reshape(B, L, D)[:, CHUNK:]
```

```python
import functools
import math

import jax
import jax.numpy as jnp
from jax import lax
from jax.experimental import pallas as pl
from jax.experimental.pallas import tpu as pltpu

N_META = 16
CHUNK = 128
PAD_FRONT = CHUNK - N_META
A_QK_DIM = 64
A_V_DIM = 128
R_QK_DIM = 128
R_V_DIM = 256
CONV_W = 3
EPS = 1e-6
NEG_INF = -1e30
CONV_HALO = 8
FFN_SUB = 256

V7X_VMEM_BYTES = 64 * 1024 * 1024
VMEM_LIMIT_BYTES = V7X_VMEM_BYTES - 8 * 1024 * 1024

F32 = jnp.float32
BF16 = jnp.bfloat16


def _pick(n, candidates):
    for c in candidates:
        if n % c == 0:
            return c
    raise ValueError(f"no tile in {candidates} divides {n}")


def _params(*semantics):
    return pltpu.CompilerParams(dimension_semantics=semantics,
                                vmem_limit_bytes=VMEM_LIMIT_BYTES)


def _rms(xf, g):
    return xf * lax.rsqrt(jnp.mean(xf * xf, axis=-1, keepdims=True) + EPS) * g


def _masked_norm(hn, g, seq_row0):
    row = lax.broadcasted_iota(jnp.int32, hn.shape, 0) + seq_row0
    return jnp.where(row >= PAD_FRONT, _rms(hn, g), 0.0)


def _prep_kernel(x_ref, meta_ref, g_ref, h_ref, u_ref):
    i = pl.program_id(1)
    g = g_ref[...]

    @pl.when(i == 0)
    def _():
        meta = meta_ref[...]
        h_ref[0, :PAD_FRONT, :] = jnp.zeros((PAD_FRONT, meta.shape[1]), F32)
        h_ref[0, PAD_FRONT:, :] = meta
        u_ref[0, :PAD_FRONT, :] = jnp.zeros((PAD_FRONT, meta.shape[1]), BF16)
        u_ref[0, PAD_FRONT:, :] = _rms(meta, g).astype(BF16)

    @pl.when(i > 0)
    def _():
        x = x_ref[0]
        h_ref[0] = x
        u_ref[0] = _rms(x, g).astype(BF16)


def _prep(x, meta, g):
    B, S, D = x.shape
    L = CHUNK + S
    return pl.pallas_call(
        _prep_kernel,
        grid=(B, L // CHUNK),
        in_specs=[
            pl.BlockSpec((1, CHUNK, D), lambda b, i: (b, jnp.maximum(i - 1, 0), 0)),
            pl.BlockSpec((N_META, D), lambda b, i: (0, 0)),
            pl.BlockSpec((1, D), lambda b, i: (0, 0)),
        ],
        out_specs=[
            pl.BlockSpec((1, CHUNK, D), lambda b, i: (b, i, 0)),
            pl.BlockSpec((1, CHUNK, D), lambda b, i: (b, i, 0)),
        ],
        out_shape=[jax.ShapeDtypeStruct((B, L, D), F32),
                   jax.ShapeDtypeStruct((B, L, D), BF16)],
        compiler_params=_params("parallel", "arbitrary"),
    )(x, meta, g)


def _matmul_kernel(x_ref, w_ref, cs_ref, *rest):
    n_cast = (len(rest) - 1) // 2
    o_ref = rest[n_cast]
    acc = jnp.dot(x_ref[...], w_ref[...], preferred_element_type=F32)
    o_ref[...] = (acc * cs_ref[...]).astype(o_ref.dtype)
    for src, dst in zip(rest[:n_cast], rest[n_cast + 1:]):
        dst[...] = src[0].astype(dst.dtype)


def _cast_block_rows(rows, steps):
    for br in range(16, rows + 1, 16):
        if rows % br == 0 and rows // br <= steps:
            return br
    raise ValueError(f"cannot split {rows} rows over {steps} steps")


def _in_proj(u, w, col_scale, layer, f32_weights):
    R, D = u.shape
    N = w.shape[1]
    tm = _pick(R, (1408, 1024, 768, 512, 384, 256, 128))
    tn = _pick(N, (1024, 512, 256, 128))
    n_tiles = N // tn
    steps = (R // tm) * n_tiles
    in_specs = [pl.BlockSpec((tm, D), lambda m, n: (m, 0)),
                pl.BlockSpec((D, tn), lambda m, n: (0, n)),
                pl.BlockSpec((1, tn), lambda m, n: (0, n))]
    out_specs = [pl.BlockSpec((tm, tn), lambda m, n: (m, n))]
    out_shape = [jax.ShapeDtypeStruct((R, N), BF16)]
    for wf in f32_weights:
        _, rows, cols = wf.shape
        br = _cast_block_rows(rows, steps)
        last = rows // br - 1
        in_specs.append(pl.BlockSpec(
            (1, br, cols), lambda m, n, last=last: (layer, jnp.minimum(m * n_tiles + n, last), 0)))
        out_specs.append(pl.BlockSpec(
            (br, cols), lambda m, n, last=last: (jnp.minimum(m * n_tiles + n, last), 0)))
        out_shape.append(jax.ShapeDtypeStruct((rows, cols), BF16))
    return pl.pallas_call(
        _matmul_kernel,
        grid=(R // tm, n_tiles),
        in_specs=in_specs,
        out_specs=out_specs,
        out_shape=out_shape,
        compiler_params=_params("arbitrary", "arbitrary"),
    )(u, w, col_scale, *f32_weights)


ATTN_TQ = 128
ATTN_TK = 512
BIAS_LANES = 4
LOG2E = math.log2(math.e)


def _attn_kernel(slopes_ref, lq_ref, subln_ref, q_ref, k_ref, v_ref, o_ref,
                 vt_ref, kb_ref, q2t_ref, acc_ref, l_ref, m_ref, *, n_heads, lam_init):
    tq, tk = ATTN_TQ, ATTN_TK
    hd = A_V_DIM
    qi = pl.program_id(1)
    L = k_ref.shape[1]
    heads = range(n_heads)
    cols = [slice(g * hd, (g + 1) * hd) for g in heads]

    @pl.when(qi == 0)
    def _():
        def vt_block(j, carry):
            r0 = pl.multiple_of(j * hd, hd)
            for g in heads:
                vt_ref[g, j] = v_ref[0, pl.ds(r0, hd), cols[g]].astype(F32).T.astype(BF16)
            return carry
        lax.fori_loop(0, L // hd, vt_block, 0)
        kpos = lax.broadcasted_iota(jnp.int32, (L, hd), 0)
        lane = lax.broadcasted_iota(jnp.int32, (L, hd), 1)
        slope = jnp.zeros((L, hd), F32)
        for g in heads:
            slope = jnp.where(lane // BIAS_LANES == g, slopes_ref[g] * LOG2E, slope)
        rest = kpos.astype(F32) * slope
        term = lane % BIAS_LANES
        kb = jnp.where((term == BIAS_LANES - 1) & (lane < n_heads * BIAS_LANES) & (kpos < PAD_FRONT),
                       NEG_INF, 0.0)
        for t in range(BIAS_LANES - 1):
            part = rest.astype(BF16).astype(F32)
            kb = jnp.where(term == t, part, kb)
            rest = rest - part
        kb_ref[...] = kb.astype(BF16)
        row = lax.broadcasted_iota(jnp.int32, (hd, 2 * tq), 0)
        for g in heads:
            q2t_ref[g, hd:, :] = jnp.where(row // BIAS_LANES == g, 1.0, 0.0).astype(BF16)

    dim = lax.broadcasted_iota(jnp.int32, (hd, tq), 0)
    for g in heads:
        qt = q_ref[0, :, cols[g]].astype(F32).T.astype(BF16)
        zero = jnp.zeros_like(qt)
        q2t_ref[g, :hd, :] = jnp.concatenate([jnp.where(dim < A_QK_DIM, qt, zero),
                                              jnp.where(dim >= A_QK_DIM, qt, zero)], axis=1)

    m_ref[...] = jnp.full(m_ref.shape, NEG_INF, F32)
    l_ref[...] = jnp.zeros(l_ref.shape, F32)
    acc_ref[...] = jnp.zeros(acc_ref.shape, F32)

    def tile(k0, width, causal):
        blk0 = k0 // hd
        if causal:
            kpos = lax.broadcasted_iota(jnp.int32, (width, 2 * tq), 0) + k0
            lane = lax.broadcasted_iota(jnp.int32, (width, 2 * tq), 1)
            qpos = jnp.where(lane >= tq, lane - tq, lane) + qi * tq
            visible = kpos <= qpos
        kb = kb_ref[pl.ds(k0, width), :]
        scores = [jnp.dot(jnp.concatenate([k_ref[0, pl.ds(k0, width), cols[g]], kb], axis=1),
                          q2t_ref[g], preferred_element_type=F32) for g in heads]
        probs, alphas = [], []
        for g in heads:
            t = scores[g]
            if causal:
                t = jnp.where(visible, t, NEG_INF)
            m_old = m_ref[g]
            m_new = jnp.maximum(m_old, jnp.max(t, axis=0, keepdims=True))
            alpha = jnp.exp2(m_old - m_new)
            p = jnp.exp2(t - m_new)
            l_ref[g] = alpha * l_ref[g] + jnp.sum(p, axis=0, keepdims=True)
            m_ref[g] = m_new
            probs.append(p.astype(BF16))
            alphas.append(alpha)
        for g in heads:
            vt = jnp.concatenate([vt_ref[g, blk0 + i] for i in range(width // hd)], axis=1)
            acc_ref[g] = alphas[g] * acc_ref[g] + jnp.dot(vt, probs[g],
                                                          preferred_element_type=F32)

    n_full = (qi * tq) // tk

    def body(j, carry):
        tile(pl.multiple_of(j * tk, tk), tk, causal=False)
        return carry

    lax.fori_loop(0, n_full, body, 0)
    tail0 = pl.multiple_of(n_full * tk, tk)
    tiles_left = qi - n_full * (tk // tq)
    for r in range(tk // tq):
        @pl.when(tiles_left == r)
        def _(r=r):
            tile(tail0, (r + 1) * tq, causal=True)

    lq = lq_ref[...]
    lam = (jnp.exp(jnp.sum(lq[0:1] * lq[1:2], keepdims=True))
           - jnp.exp(jnp.sum(lq[2:3] * lq[3:4], keepdims=True)) + lam_init)
    subln = subln_ref[...]
    for g in heads:
        out_t = acc_ref[g] * (1.0 / l_ref[g])
        a = (out_t[:, :tq] - lam * out_t[:, tq:]).T
        y = _rms(a, subln) * (1.0 - lam_init)
        o_ref[0, :, cols[g]] = y.astype(o_ref.dtype)


def _attention(proj, slopes, lq, subln, lam_init, n_heads):
    B, L, _ = proj.shape
    tq, tk = ATTN_TQ, ATTN_TK
    hd = A_V_DIM
    d_a = n_heads * hd
    kern = functools.partial(_attn_kernel, n_heads=n_heads, lam_init=lam_init)
    return pl.pallas_call(
        kern,
        grid=(B, L // tq),
        in_specs=[
            pl.BlockSpec(memory_space=pltpu.SMEM),
            pl.BlockSpec((4, A_QK_DIM), lambda b, i: (0, 0)),
            pl.BlockSpec((1, hd), lambda b, i: (0, 0)),
            pl.BlockSpec((1, tq, d_a), lambda b, i: (b, i, 0)),
            pl.BlockSpec((1, L, d_a), lambda b, i: (b, 0, 1)),
            pl.BlockSpec((1, L, d_a), lambda b, i: (b, 0, 2), pipeline_mode=pl.Buffered(1)),
        ],
        out_specs=pl.BlockSpec((1, tq, d_a), lambda b, i: (b, i, 0)),
        out_shape=jax.ShapeDtypeStruct((B, L, d_a), BF16),
        scratch_shapes=[pltpu.VMEM((n_heads, L // hd, hd, hd), BF16),
                        pltpu.VMEM((L, hd), BF16),
                        pltpu.VMEM((n_heads, 2 * hd, 2 * tq), BF16),
                        pltpu.VMEM((n_heads, hd, 2 * tq), F32),
                        pltpu.VMEM((n_heads, 1, 2 * tq), F32),
                        pltpu.VMEM((n_heads, 1, 2 * tq), F32)],
        compiler_params=_params("parallel", "arbitrary"),
    )(slopes, lq, subln, proj, proj, proj)


def _ret_kernel(logg_ref, rn_ref, q_ref, k_ref, v_ref, g_ref, o_ref, s_ref,
                *, n_heads, n_chunks):
    @pl.when(pl.program_id(1) == 0)
    def _():
        s_ref[...] = jnp.zeros(s_ref.shape, F32)

    C = CHUNK
    ri = lax.broadcasted_iota(jnp.int32, (C, C), 0)
    ci = lax.broadcasted_iota(jnp.int32, (C, C), 1)
    diff = (ri - ci).astype(F32)
    idx = lax.broadcasted_iota(jnp.int32, (C, 1), 0).astype(F32)
    scale = R_QK_DIM ** -0.5
    rn = rn_ref[...]

    for h in range(n_heads):
        lg = logg_ref[h]
        decay_in = jnp.where(diff >= 0, jnp.exp(lg * jnp.maximum(diff, 0.0)), 0.0) * scale
        k_decay = jnp.exp(lg * (C - 1.0 - idx)) * scale
        q_decay = jnp.exp(lg * (idx + 1.0))
        g_chunk = jnp.exp(jnp.full((1, 1), lg * C, F32))
        qk = slice(h * R_QK_DIM, (h + 1) * R_QK_DIM)
        vv = slice(h * R_V_DIM, (h + 1) * R_V_DIM)
        for c in range(n_chunks):
            rows = slice(c * C, (c + 1) * C)
            q = q_ref[0, rows, qk]
            k = k_ref[0, rows, qk]
            v = v_ref[0, rows, vv]
            s = lax.dot_general(q, k, (((1,), (1,)), ((), ())),
                                preferred_element_type=F32) * decay_in
            intra = jnp.dot(s.astype(BF16), v, preferred_element_type=F32)
            s_prev = s_ref[h]
            q_dec = (q.astype(F32) * q_decay).astype(BF16)
            cross = jnp.dot(q_dec, s_prev.astype(BF16), preferred_element_type=F32)
            k_dec = (k.astype(F32) * k_decay).astype(BF16)
            kv = lax.dot_general(k_dec, v, (((0,), (0,)), ((), ())),
                                 preferred_element_type=F32)
            s_ref[h] = g_chunk * s_prev + kv
            y = _rms(intra + cross, rn)
            gate = g_ref[0, rows, vv].astype(F32)
            o_ref[0, rows, vv] = (y * (gate * jax.nn.sigmoid(gate))).astype(o_ref.dtype)


def _retention(proj, log_g, rn, n_heads, col0):
    B, L, _ = proj.shape
    dqk = n_heads * R_QK_DIM
    dv = n_heads * R_V_DIM
    T = _pick(L, (384, 128))
    kern = functools.partial(_ret_kernel, n_heads=n_heads, n_chunks=T // CHUNK)
    q_blk = col0 // dqk
    v_blk = (col0 + 2 * dqk) // dv
    return pl.pallas_call(
        kern,
        grid=(B, L // T),
        in_specs=[
            pl.BlockSpec(memory_space=pltpu.SMEM),
            pl.BlockSpec((1, R_V_DIM), lambda b, n: (0, 0)),
            pl.BlockSpec((1, T, dqk), lambda b, n: (b, n, q_blk)),
            pl.BlockSpec((1, T, dqk), lambda b, n: (b, n, q_blk + 1)),
            pl.BlockSpec((1, T, dv), lambda b, n: (b, n, v_blk)),
            pl.BlockSpec((1, T, dv), lambda b, n: (b, n, v_blk + 1)),
        ],
        out_specs=pl.BlockSpec((1, T, dv), lambda b, n: (b, n, 0)),
        out_shape=jax.ShapeDtypeStruct((B, L, dv), BF16),
        scratch_shapes=[pltpu.VMEM((n_heads, R_QK_DIM, R_V_DIM), F32)],
        compiler_params=_params("parallel", "arbitrary"),
    )(log_g, rn, proj, proj, proj, proj)


def _out_proj_kernel(ya_ref, yr_ref, w_ref, h_ref, g_ref, ho_ref, u_ref,
                     *, tm, tiles_per_seq):
    da = ya_ref.shape[1]
    acc = jnp.dot(ya_ref[...], w_ref[:da, :], preferred_element_type=F32)
    acc = acc + jnp.dot(yr_ref[...], w_ref[da:, :], preferred_element_type=F32)
    hn = h_ref[...] + acc
    ho_ref[...] = hn
    seq_row0 = (pl.program_id(0) % tiles_per_seq) * tm
    u_ref[...] = _masked_norm(hn, g_ref[...], seq_row0).astype(u_ref.dtype)


def _out_proj(ya, yr, w, h, g, L):
    R, D = h.shape
    da, dr = ya.shape[1], yr.shape[1]
    tm = _pick(L, (528, 384, 128))
    kern = functools.partial(_out_proj_kernel, tm=tm, tiles_per_seq=L // tm)
    return pl.pallas_call(
        kern,
        grid=(R // tm,),
        in_specs=[
            pl.BlockSpec((tm, da), lambda m: (m, 0)),
            pl.BlockSpec((tm, dr), lambda m: (m, 0)),
            pl.BlockSpec((da + dr, D), lambda m: (0, 0)),
            pl.BlockSpec((tm, D), lambda m: (m, 0)),
            pl.BlockSpec((1, D), lambda m: (0, 0)),
        ],
        out_specs=[pl.BlockSpec((tm, D), lambda m: (m, 0)),
                   pl.BlockSpec((tm, D), lambda m: (m, 0))],
        out_shape=[jax.ShapeDtypeStruct((R, D), F32),
                   jax.ShapeDtypeStruct((R, D), BF16)],
        input_output_aliases={3: 0},
        compiler_params=_params("parallel"),
    )(ya, yr, w, h, g)


def _ffn_kernel(u_ref, wg_ref, wu_ref, cw_ref, wd_ref, h_ref, g_ref, *rest,
                tm, tiles_per_seq, final):
    if final:
        ho_ref, gbuf_ref, carry_ref = rest
    else:
        wnext_ref, ho_ref, u_out_ref, wnext_out_ref, gbuf_ref, carry_ref = rest
        wnext_out_ref[...] = wnext_ref[0].astype(wnext_out_ref.dtype)
    m = pl.program_id(0)
    f = pl.program_id(1)
    seq_tile = m % tiles_per_seq
    tf = wg_ref.shape[1]

    @pl.when(seq_tile == 0)
    def _():
        gbuf_ref[:CONV_HALO, :] = jnp.zeros((CONV_HALO, tf), F32)

    @pl.when(seq_tile != 0)
    def _():
        gbuf_ref[:CONV_HALO, :] = carry_ref[f]

    @pl.when(f == 0)
    def _():
        ho_ref[...] = h_ref[...]

    u = u_ref[...]
    subs = [slice(c, c + FFN_SUB) for c in range(0, tf, FFN_SUB)]
    gates = [jnp.dot(u, wg_ref[:, cs], preferred_element_type=F32) for cs in subs]
    ups = [jnp.dot(u, wu_ref[:, cs], preferred_element_type=F32) for cs in subs]
    for cs, gate, up in zip(subs, gates, ups):
        gbuf_ref[CONV_HALO:, cs] = gate
        carry_ref[f, :, cs] = gate[tm - CONV_HALO:, :]
        cw = cw_ref[:, cs]
        gc = gate * cw[CONV_W - 1:CONV_W, :]
        for i in range(CONV_W - 1):
            back = CONV_W - 1 - i
            gc = gc + gbuf_ref[pl.ds(CONV_HALO - back, tm), cs] * cw[i:i + 1, :]
        act = (gc * jax.nn.sigmoid(gc)) * up
        ho_ref[...] += jnp.dot(act.astype(BF16), wd_ref[cs, :], preferred_element_type=F32)

    @pl.when(f == pl.num_programs(1) - 1)
    def _():
        hn = ho_ref[...]
        g = g_ref[...]
        if final:
            ho_ref[...] = _rms(hn, g)
        else:
            u_out_ref[...] = _masked_norm(hn, g, seq_tile * tm).astype(u_out_ref.dtype)


def _ffn(u, wg, wu, cw, wd, h, g, L, w_in_f32, next_layer):
    R, D = h.shape
    F = wg.shape[1]
    final = next_layer is None
    tm = _pick(L, (704, 384, 128))
    tf = _pick(F, (512, 256, 128))
    n_f = F // tf
    kern = functools.partial(_ffn_kernel, tm=tm, tiles_per_seq=L // tm, final=final)
    row_spec = pl.BlockSpec((tm, D), lambda m, f: (m, 0))
    in_specs = [
        row_spec,
        pl.BlockSpec((D, tf), lambda m, f: (0, f)),
        pl.BlockSpec((D, tf), lambda m, f: (0, f)),
        pl.BlockSpec((CONV_W, tf), lambda m, f: (0, f)),
        pl.BlockSpec((tf, D), lambda m, f: (f, 0)),
        row_spec,
        pl.BlockSpec((1, D), lambda m, f: (0, 0)),
    ]
    operands = [u, wg, wu, cw, wd, h, g]
    out_specs = [row_spec]
    out_shape = [jax.ShapeDtypeStruct((R, D), F32)]
    if not final:
        _, rows, cols = w_in_f32.shape
        br = _cast_block_rows(rows, (R // tm) * n_f)
        last = rows // br - 1
        in_specs.append(pl.BlockSpec(
            (1, br, cols), lambda m, f: (next_layer, jnp.minimum(m * n_f + f, last), 0)))
        operands.append(w_in_f32)
        out_specs += [row_spec,
                      pl.BlockSpec((br, cols), lambda m, f: (jnp.minimum(m * n_f + f, last), 0))]
        out_shape += [jax.ShapeDtypeStruct((R, D), BF16),
                      jax.ShapeDtypeStruct((rows, cols), BF16)]
    return pl.pallas_call(
        kern,
        grid=(R // tm, n_f),
        in_specs=in_specs,
        out_specs=out_specs,
        out_shape=out_shape,
        scratch_shapes=[pltpu.VMEM((tm + CONV_HALO, tf), F32),
                        pltpu.VMEM((n_f, CONV_HALO, tf), F32)],
        input_output_aliases={5: 0},
        compiler_params=_params("arbitrary", "arbitrary"),
    )(*operands)


def kernel(x, meta_tokens, attn_norm, w_in, lambda_qk, attn_subln, ret_norm, w_out,
           ffn_norm, w_gate, w_up, conv_w, w_down, final_norm):
    B, S, D = x.shape
    L = CHUNK + S
    R = B * L
    depth = w_in.shape[0]
    a_heads = D // 256
    r_heads = D // 512
    d_a = a_heads * A_V_DIM
    ret_col0 = 2 * a_heads * 2 * A_QK_DIM + d_a

    slopes = 2.0 ** (-8.0 * jnp.arange(1, a_heads + 1, dtype=F32) / a_heads)
    log_g = jnp.log1p(-(2.0 ** (-5.0 - jnp.arange(r_heads, dtype=F32))))

    n_aq = a_heads * 2 * A_QK_DIM
    col_scale = jnp.where(jnp.arange(w_in.shape[2]) < n_aq,
                          LOG2E * A_QK_DIM ** -0.5, 1.0).astype(F32)[None, :]

    w_in_b = w_in[0].astype(BF16)

    h, u = _prep(x, meta_tokens, attn_norm[0:1])
    h = h.reshape(R, D)
    u = u.reshape(R, D)
    for l in range(depth):
        lam_init = 0.8 - 0.6 * math.exp(-0.3 * l)
        proj, w_gate_b, w_up_b, w_down_b, w_out_b = _in_proj(
            u, w_in_b, col_scale, l, (w_gate, w_up, w_down, w_out))
        proj = proj.reshape(B, L, -1)
        ya = _attention(proj, slopes, lambda_qk[l], attn_subln[l:l + 1], lam_init, a_heads)
        yr = _retention(proj, log_g, ret_norm[l:l + 1], r_heads, ret_col0)
        h, u2 = _out_proj(ya.reshape(R, -1), yr.reshape(R, -1), w_out_b, h,
                          ffn_norm[l:l + 1], L)
        if l == depth - 1:
            (h,) = _ffn(u2, w_gate_b, w_up_b, conv_w[l], w_down_b, h, final_norm[None, :], L,
                        w_in, None)
        else:
            h, u, w_in_b = _ffn(u2, w_gate_b, w_up_b, conv_w[l], w_down_b, h,
                                attn_norm[l + 1:l + 2], L, w_in, l + 1)
    return h.reshape(B, L, D)[:, CHUNK:]
```
